```python
import jax, jax.numpy as jnp
from jax import lax
import numpy as np

D_MODEL = 2048
BATCH = 4
SEQ = 2048
DEPTH = 4

CHUNK = 64
EPS = 1e-6
RET_HEADS = 8
RET_DK = 256
RET_DV = 256
RET_W = RET_HEADS * RET_DV
ROT_BASE = 10000.0
ATT_HEADS = 16
ATT_DH = 128
ATT_W = ATT_HEADS * ATT_DH
IDX_HEADS = 16
IDX_DH = 64
INDEX_TOPK = 256
Q_BLOCK = 128

SPLIT_SIZES = (RET_HEADS * RET_DK, RET_HEADS * RET_DK, RET_W, RET_W,
               ATT_W, ATT_DH, ATT_DH, ATT_W,
               IDX_HEADS * IDX_DH, IDX_DH, IDX_HEADS,
               D_MODEL, D_MODEL)
N_IN = sum(SPLIT_SIZES)

kernel_name = "hybrid_retention_dsa_gated_trunk"


def rmsnorm(x, g):
    xf = x.astype(jnp.float32)
    r = lax.rsqrt(jnp.mean(xf * xf, axis=-1, keepdims=True) + EPS)
    return (xf * r * g.astype(jnp.float32)).astype(x.dtype)


def rotate(x, cos, sin):
    half = x.shape[-1] // 2
    x1, x2 = x[..., :half], x[..., half:]
    return jnp.concatenate([x1 * cos - x2 * sin, x2 * cos + x1 * sin], axis=-1)


def retention(q, k, v):
    B, S, H, DK = q.shape
    DV = v.shape[-1]
    dt = q.dtype
    nc = S // CHUNK
    pos = jnp.arange(S, dtype=jnp.float32)
    inv = 1.0 / (ROT_BASE ** jnp.linspace(0.0, 1.0, DK // 2, dtype=jnp.float32))
    ang = pos[:, None] * inv[None, :]
    cos = jnp.cos(ang)[None, :, None, :].astype(dt)
    sin = jnp.sin(ang)[None, :, None, :].astype(dt)
    q = rotate(q, cos, sin)
    k = rotate(k, cos, sin) * jnp.asarray(DK ** -0.5, dt)
    log_g = jnp.log(1.0 - 2.0 ** (-5.0 - jnp.arange(H, dtype=jnp.float32)))
    idx = jnp.arange(CHUNK, dtype=jnp.float32)
    intra = jnp.exp(jnp.abs(idx[:, None] - idx[None, :])[None] * log_g[:, None, None]).astype(dt)
    q_dec = jnp.exp((idx[:, None] + 1.0) * log_g[None, :]).astype(dt)
    k_dec = jnp.exp((CHUNK - 1.0 - idx[:, None]) * log_g[None, :]).astype(dt)
    c_dec = jnp.exp(CHUNK * log_g).astype(dt)

    def to_chunks(a):
        return a.reshape(B, nc, CHUNK, H, a.shape[-1]).transpose(1, 0, 2, 3, 4)

    def step(state, inp):
        qc, kc, vc = inp
        s = jnp.einsum('bihd,bjhd->bhij', qc, kc) * intra[None]
        o = jnp.einsum('bhij,bjhv->bihv', s, vc)
        o = o + jnp.einsum('bihd,bhdv->bihv', qc, state) * q_dec[None, :, :, None]
        state = state * c_dec[None, :, None, None] + jnp.einsum(
            'bjhd,bjhv->bhdv', kc * k_dec[None, :, :, None], vc)
        return state, o

    state0 = jnp.zeros((B, H, DK, DV), dt)
    _, o = lax.scan(step, state0, (to_chunks(q), to_chunks(k), to_chunks(v)))
    return o.transpose(1, 0, 2, 3, 4).reshape(B, S, H, DV)


def sparse_attention(q, k, v, qi, ki, wi):
    B, S, HA, DH = q.shape
    L = k.shape[1]
    topk = min(INDEX_TOPK, L // 4)
    nb = S // Q_BLOCK
    key_chunk = jnp.arange(L) // CHUNK
    idx_scale = IDX_DH ** -0.5
    att_scale = DH ** -0.5

    def blocks(a):
        return a.reshape((B, nb, Q_BLOCK) + a.shape[2:]).swapaxes(0, 1)

    def one_block(args):
        qb, qib, wib, tb = args
        sc = jax.nn.relu(jnp.einsum('bthd,bsd->btsh', qib, ki))
        score = jnp.einsum('btsh,bth->bts', sc, wib).astype(jnp.float32) * idx_scale
        adm = key_chunk[None, :] <= (tb // CHUNK)[:, None]
        score = jnp.where(adm[None], score, -jnp.inf)
        vals, sel = lax.top_k(score, topk)
        valid = jnp.isfinite(vals)
        ks = jax.vmap(lambda a, i: a[i])(k, sel)
        vs = jax.vmap(lambda a, i: a[i])(v, sel)
        logits = jnp.einsum('bthd,btkd->bhtk', qb, ks).astype(jnp.float32) * att_scale
        logits = jnp.where(valid[:, None], logits, -jnp.inf)
        p = jax.nn.softmax(logits, axis=-1).astype(vs.dtype)
        return jnp.einsum('bhtk,btkd->bthd', p, vs)

    tpos = jnp.arange(S, dtype=jnp.int32).reshape(nb, Q_BLOCK)
    o = lax.map(one_block, (blocks(q), blocks(qi), blocks(wi), tpos))
    return o.swapaxes(0, 1).reshape(B, S, HA, DH)


def setup_inputs(seed: int = 0) -> dict:
    key = jax.random.key(seed)
    ks = jax.random.split(key, 10)
    f32 = jnp.float32
    x = jax.random.normal(ks[0], (BATCH, SEQ, D_MODEL), f32)
    norm_g = 1.0 + 0.02 * jax.random.normal(ks[1], (DEPTH, D_MODEL), f32)
    w_in = jax.random.normal(ks[2], (DEPTH, D_MODEL, N_IN), f32) * D_MODEL ** -0.5
    ret_out_g = 1.0 + 0.02 * jax.random.normal(ks[3], (DEPTH, RET_HEADS, RET_DV), f32)
    att_q_g = 1.0 + 0.02 * jax.random.normal(ks[4], (DEPTH, ATT_DH), f32)
    att_k_g = 1.0 + 0.02 * jax.random.normal(ks[5], (DEPTH, ATT_DH), f32)
    idx_k_g = 1.0 + 0.02 * jax.random.normal(ks[6], (DEPTH, IDX_DH), f32)
    w_branch_ret = jax.random.normal(ks[7], (DEPTH, RET_W, D_MODEL), f32) * RET_W ** -0.5
    w_branch_att = jax.random.normal(ks[8], (DEPTH, ATT_W, D_MODEL), f32) * ATT_W ** -0.5
    w_out = jax.random.normal(ks[9], (DEPTH, D_MODEL, D_MODEL), f32) * D_MODEL ** -0.5
    return {"x": x, "norm_g": norm_g, "w_in": w_in, "ret_out_g": ret_out_g,
            "att_q_g": att_q_g, "att_k_g": att_k_g, "idx_k_g": idx_k_g,
            "w_branch_ret": w_branch_ret, "w_branch_att": w_branch_att, "w_out": w_out}


def reference(x, norm_g, w_in, ret_out_g, att_q_g, att_k_g, idx_k_g,
              w_branch_ret, w_branch_att, w_out):
    B, S, _ = x.shape
    offsets = []
    acc = 0
    for sz in SPLIT_SIZES[:-1]:
        acc += sz
        offsets.append(acc)
    for l in range(DEPTH):
        h = rmsnorm(x, norm_g[l])
        p = h @ w_in[l]
        (rq, rk, rv, rg, aq, ak, av, ag, iq, ik, iw, ga, gb) = jnp.split(p, offsets, axis=-1)
        ro = retention(rq.reshape(B, S, RET_HEADS, RET_DK),
                       rk.reshape(B, S, RET_HEADS, RET_DK),
                       rv.reshape(B, S, RET_HEADS, RET_DV))
        ro = rmsnorm(ro, ret_out_g[l]).reshape(B, S, RET_W)
        u_ret = (ro * jax.nn.silu(rg)) @ w_branch_ret[l]
        qa = rmsnorm(aq.reshape(B, S, ATT_HEADS, ATT_DH), att_q_g[l])
        ka = rmsnorm(ak, att_k_g[l])
        ao = sparse_attention(qa, ka, av,
                              iq.reshape(B, S, IDX_HEADS, IDX_DH),
                              rmsnorm(ik, idx_k_g[l]),
                              iw * jnp.asarray(IDX_HEADS ** -0.5, iw.dtype))
        u_att = (ao.reshape(B, S, ATT_W) * jax.nn.silu(ag)) @ w_branch_att[l]
        m = jax.nn.sigmoid(ga) * u_ret + jax.nn.sigmoid(gb) * u_att
        x = x + m @ w_out[l]
    return x
```

```python
import functools

import jax
import jax.numpy as jnp
from jax import lax
from jax.experimental import pallas as pl
from jax.experimental.pallas import tpu as pltpu

F32 = jnp.float32
BF16 = jnp.bfloat16

CHUNK = 64
EPS = 1e-6
RET_HEADS = 8
RET_DK = 256
RET_DV = 256
ROT_BASE = 10000.0
ATT_HEADS = 16
ATT_DH = 128
IDX_HEADS = 16
IDX_DH = 64
INDEX_TOPK = 256
Q_BLOCK = 128
LANES = 128

INT_MIN = -(2 ** 31)
KEY_NEG_INF = -2139095041
KEY_POS_INF = 2139095040
MASKED_LOGIT = -1e30

VMEM_LIMIT_BYTES = 56 * 1024 * 1024


def _params(n_axes):
    return pltpu.CompilerParams(dimension_semantics=("arbitrary",) * n_axes,
                                vmem_limit_bytes=VMEM_LIMIT_BYTES)


def _tile(n, pref):
    t = min(n, pref)
    assert n % t == 0, (n, t)
    return t


def _silu(g):
    return g * jax.nn.sigmoid(g)


def _rmsnorm_kernel(x_ref, g_ref, h_ref):
    x = x_ref[...]
    r = lax.rsqrt(jnp.mean(x * x, axis=-1, keepdims=True) + EPS)
    h_ref[...] = (x * r * g_ref[...]).astype(h_ref.dtype)


def _rmsnorm(x, norm_g, layer):
    T, D = x.shape
    tm = _tile(T, 512)
    return pl.pallas_call(
        _rmsnorm_kernel,
        grid=(T // tm,),
        in_specs=[pl.BlockSpec((tm, D), lambda i: (i, 0)),
                  pl.BlockSpec((None, 1, D), lambda i: (layer, 0, 0))],
        out_specs=pl.BlockSpec((tm, D), lambda i: (i, 0)),
        out_shape=jax.ShapeDtypeStruct((T, D), BF16),
        compiler_params=_params(1),
        name="rmsnorm",
    )(x, norm_g)


def _matmul_kernel(a_ref, b_ref, o_ref):
    o_ref[...] = jnp.dot(a_ref[...], b_ref[...], preferred_element_type=F32).astype(o_ref.dtype)


def _project(h, w, layer, tn, name):
    T, K = h.shape
    N = w.shape[-1]
    tm = _tile(T, 1024)
    return pl.pallas_call(
        _matmul_kernel,
        grid=(N // tn, T // tm),
        in_specs=[pl.BlockSpec((tm, K), lambda j, i: (i, 0)),
                  pl.BlockSpec((None, K, tn), lambda j, i: (layer, 0, j))],
        out_specs=pl.BlockSpec((tm, tn), lambda j, i: (i, j)),
        out_shape=jax.ShapeDtypeStruct((T, N), BF16),
        compiler_params=_params(2),
        name=name,
    )(h, w)


def _retention_tables(S, R):
    pos = jnp.arange(S, dtype=F32)
    inv = 1.0 / (ROT_BASE ** jnp.linspace(0.0, 1.0, RET_DK // 2, dtype=F32))
    ang = pos[:, None] * inv[None, :]
    log_g = jnp.log(1.0 - 2.0 ** (-5.0 - jnp.arange(RET_HEADS, dtype=F32)))[:, None, None]
    idx = jnp.arange(R, dtype=F32)
    chunk = jnp.arange(R) // CHUNK
    visible = chunk[None, :] <= chunk[:, None]
    dm = jnp.where(visible[None], jnp.exp(jnp.abs(idx[:, None] - idx[None, :])[None] * log_g), 0.0)
    ones = jnp.ones((1, 1, RET_DV), F32)
    qd = jnp.exp((idx[None, :, None] + 1.0) * log_g) * ones
    kd = jnp.exp((R - 1.0 - idx)[None, :, None] * log_g) * ones
    cd = jnp.exp(float(R) * log_g) * ones
    return jnp.cos(ang), jnp.sin(ang), dm, qd, kd, cd


def _retention_kernel(q_ref, k_ref, v_ref, g_ref, cos_ref, sin_ref, dm_ref, qd_ref, kd_ref, cd_ref,
                      og_ref, o_ref, state_ref):
    @pl.when(pl.program_id(1) == 0)
    def _():
        state_ref[...] = jnp.zeros_like(state_ref)

    cos = cos_ref[...]
    sin = sin_ref[...]
    half = RET_DK // 2

    def rotated(ref, h):
        x1 = ref[:, h * RET_DK:h * RET_DK + half].astype(F32)
        x2 = ref[:, h * RET_DK + half:(h + 1) * RET_DK].astype(F32)
        return jnp.concatenate([x1 * cos - x2 * sin, x2 * cos + x1 * sin], axis=1)

    for h in range(RET_HEADS):
        cols = slice(h * RET_DV, (h + 1) * RET_DV)
        qr = rotated(q_ref, h)
        kr = rotated(k_ref, h) * (RET_DK ** -0.5)
        qb = qr.astype(BF16)
        kb = kr.astype(BF16)
        v = v_ref[:, cols]
        s = lax.dot_general(qb, kb, (((1,), (1,)), ((), ())), preferred_element_type=F32) * dm_ref[h]
        o = jnp.dot(s.astype(BF16), v, preferred_element_type=F32)
        state = state_ref[h]
        o = o + jnp.dot(qb, state.astype(BF16), preferred_element_type=F32) * qd_ref[h]
        kdec = (kr * kd_ref[h]).astype(BF16)
        state_ref[h] = state * cd_ref[h] + lax.dot_general(
            kdec, v, (((0,), (0,)), ((), ())), preferred_element_type=F32)
        r = lax.rsqrt(jnp.mean(o * o, axis=-1, keepdims=True) + EPS)
        g = g_ref[:, cols].astype(F32)
        o_ref[:, cols] = (o * r * og_ref[h] * _silu(g)).astype(o_ref.dtype)


def _retention(p_main, tables, ret_out_g, layer, B, S):
    T = p_main.shape[0]
    W = RET_HEADS * RET_DV
    R = tables[2].shape[1]
    nr = S // R
    cos, sin, dm, qd, kd, cd = tables
    row = lambda b, i: b * nr + i
    full3 = lambda a: pl.BlockSpec(a.shape, lambda b, i: (0, 0, 0))
    return pl.pallas_call(
        _retention_kernel,
        grid=(B, nr),
        in_specs=[pl.BlockSpec((R, W), lambda b, i: (row(b, i), 0)),
                  pl.BlockSpec((R, W), lambda b, i: (row(b, i), 1)),
                  pl.BlockSpec((R, W), lambda b, i: (row(b, i), 2)),
                  pl.BlockSpec((R, W), lambda b, i: (row(b, i), 3)),
                  pl.BlockSpec((R, RET_DK // 2), lambda b, i: (i, 0)),
                  pl.BlockSpec((R, RET_DK // 2), lambda b, i: (i, 0)),
                  full3(dm), full3(qd), full3(kd), full3(cd),
                  pl.BlockSpec((None, RET_HEADS, 1, RET_DV), lambda b, i: (layer, 0, 0, 0))],
        out_specs=pl.BlockSpec((R, W), lambda b, i: (row(b, i), 0)),
        out_shape=jax.ShapeDtypeStruct((T, W), BF16),
        scratch_shapes=[pltpu.VMEM((RET_HEADS, RET_DK, RET_DV), F32)],
        compiler_params=_params(2),
        name="retention",
    )(p_main, p_main, p_main, p_main, cos, sin, dm, qd, kd, cd, ret_out_g)


def _sparse_attention_kernel(iq_ref, iw_ref, aq_ref, ag_ref, ak_ref, av_ref, ik_ref, gq_ref, gk_ref, gik_ref,
                             o_ref, kn_ref, vt_ref, ikn_ref, key_ref, bias_ref, *, topk):
    S = ak_ref.shape[0]
    TQ = iq_ref.shape[0]
    i = pl.program_id(1)

    @pl.when(i == 0)
    def _normalise_keys():
        ak = ak_ref[...].astype(F32)
        r = lax.rsqrt(jnp.mean(ak * ak, axis=-1, keepdims=True) + EPS)
        kn_ref[...] = (ak * r * gk_ref[...]).astype(BF16)
        vt_ref[...] = av_ref[...].astype(F32).T.astype(BF16)
        ik = ik_ref[...].astype(F32)
        r = lax.rsqrt(jnp.sum(ik * ik, axis=-1, keepdims=True) * (1.0 / IDX_DH) + EPS)
        ikn = ik * r * gik_ref[...]
        ikn_ref[0] = ikn.astype(BF16)
        ikn_ref[1] = pltpu.roll(ikn, IDX_DH, 1).astype(BF16)

    iqT = iq_ref[...].astype(F32).T.astype(BF16)
    wT = iw_ref[...].astype(F32).T * (IDX_HEADS ** -0.5 * IDX_DH ** -0.5)
    score = jnp.zeros((S, TQ), F32)
    for hp in range(IDX_HEADS // 2):
        slab = iqT[hp * LANES:(hp + 1) * LANES, :]
        s0 = jnp.dot(ikn_ref[0], slab, preferred_element_type=F32)
        s1 = jnp.dot(ikn_ref[1], slab, preferred_element_type=F32)
        score = score + jnp.maximum(s0, 0.0) * wT[2 * hp:2 * hp + 1, :]
        score = score + jnp.maximum(s1, 0.0) * wT[2 * hp + 1:2 * hp + 2, :]

    bits = lax.bitcast_convert_type(score, jnp.int32)
    key = bits ^ ((bits >> 31) & 0x7FFFFFFF)
    kpos = lax.broadcasted_iota(jnp.int32, (S, TQ), 0)
    qpos = lax.broadcasted_iota(jnp.int32, (S, TQ), 1) + i * TQ
    key = jnp.where(kpos // CHUNK <= qpos // CHUNK, key, INT_MIN)
    key_ref[...] = key

    def count(pred):
        return jnp.sum(jnp.where(pred, 1.0, 0.0), axis=0, keepdims=True)

    def try_threshold(cand, thr, cnt):
        c = count(key_ref[...] >= cand)
        ok = c >= topk
        return jnp.where(ok, cand, thr), jnp.where(ok, c, cnt)

    thr = jnp.full((1, TQ), INT_MIN, jnp.int32)
    cnt = jnp.full((1, TQ), float(S), F32)
    thr, cnt = try_threshold(jnp.zeros((1, TQ), jnp.int32), thr, cnt)

    def bit_step(it, carry):
        thr, cnt = carry
        return try_threshold(thr | (1 << (30 - it)), thr, cnt)

    thr, cnt = lax.fori_loop(0, 31, bit_step, (thr, cnt))

    key = key_ref[...]
    finite = (key > KEY_NEG_INF) & (key < KEY_POS_INF)
    bias_ref[...] = jnp.where((key >= thr) & finite, 0.0, MASKED_LOGIT)

    tied = jnp.where((cnt > topk) & (thr > KEY_NEG_INF), 1.0, 0.0)

    @pl.when(jnp.max(tied) > 0.0)
    def _break_ties():
        key = key_ref[...]
        above = key > thr
        equal = key == thr
        need = topk - count(above)
        kpos = lax.broadcasted_iota(jnp.int32, (S, TQ), 0)

        def pos_step(it, bound):
            cand = bound | (1 << (S.bit_length() - 1 - it))
            ok = count(equal & (kpos < cand)) <= need
            return jnp.where(ok, cand, bound)

        bound = lax.fori_loop(0, S.bit_length(), pos_step, jnp.zeros((1, TQ), jnp.int32))
        finite = (key > KEY_NEG_INF) & (key < KEY_POS_INF)
        bias_ref[...] = jnp.where((above | (equal & (kpos < bound))) & finite, 0.0, MASKED_LOGIT)

    aqT = aq_ref[...].astype(F32).T
    gq = gq_ref[...]
    att_scale = ATT_DH ** -0.5
    for hp in range(ATT_HEADS // 2):
        heads = (2 * hp, 2 * hp + 1)
        qn = []
        for h in heads:
            x = aqT[h * ATT_DH:(h + 1) * ATT_DH, :]
            r = lax.rsqrt(jnp.mean(x * x, axis=0, keepdims=True) + EPS)
            qn.append((x * r * gq).astype(BF16))
        logits = jnp.dot(kn_ref[...], jnp.concatenate(qn, axis=1), preferred_element_type=F32)
        probs, dens = [], []
        for u in range(2):
            s = logits[:, u * TQ:(u + 1) * TQ] * att_scale + bias_ref[...]
            p = jnp.exp(s - jnp.max(s, axis=0, keepdims=True))
            dens.append(jnp.sum(p, axis=0, keepdims=True))
            probs.append(p.astype(BF16))
        oT = jnp.dot(vt_ref[...], jnp.concatenate(probs, axis=1), preferred_element_type=F32)
        for u, h in enumerate(heads):
            cols = slice(h * ATT_DH, (h + 1) * ATT_DH)
            o = (oT[:, u * TQ:(u + 1) * TQ] / dens[u]).T
            g = ag_ref[:, cols].astype(F32)
            o_ref[:, cols] = (o * _silu(g)).astype(o_ref.dtype)


def _sparse_attention(p_main, p_small, gq, gk, gik, layer, B, S):
    T = p_main.shape[0]
    W = ATT_HEADS * ATT_DH
    TQ = Q_BLOCK
    nb = S // TQ
    topk = min(INDEX_TOPK, S // 4)
    qrow = lambda b, i: b * nb + i
    n_iq = IDX_HEADS * IDX_DH
    seg = n_iq // LANES
    return pl.pallas_call(
        functools.partial(_sparse_attention_kernel, topk=topk),
        grid=(B, nb),
        in_specs=[pl.BlockSpec((TQ, n_iq), lambda b, i: (qrow(b, i), 0)),
                  pl.BlockSpec((TQ, LANES), lambda b, i: (qrow(b, i), seg + 3)),
                  pl.BlockSpec((TQ, W), lambda b, i: (qrow(b, i), 4)),
                  pl.BlockSpec((TQ, W), lambda b, i: (qrow(b, i), 5)),
                  pl.BlockSpec((S, LANES), lambda b, i: (b, seg)),
                  pl.BlockSpec((S, LANES), lambda b, i: (b, seg + 1)),
                  pl.BlockSpec((S, LANES), lambda b, i: (b, seg + 2)),
                  pl.BlockSpec((None, ATT_DH, 1), lambda b, i: (layer, 0, 0)),
                  pl.BlockSpec((None, 1, ATT_DH), lambda b, i: (layer, 0, 0)),
                  pl.BlockSpec((None, 1, LANES), lambda b, i: (layer, 0, 0))],
        out_specs=pl.BlockSpec((TQ, W), lambda b, i: (qrow(b, i), 0)),
        out_shape=jax.ShapeDtypeStruct((T, W), BF16),
        scratch_shapes=[pltpu.VMEM((S, ATT_DH), BF16),
                        pltpu.VMEM((ATT_DH, S), BF16),
                        pltpu.VMEM((2, S, LANES), BF16),
                        pltpu.VMEM((S, TQ), jnp.int32),
                        pltpu.VMEM((S, TQ), F32)],
        compiler_params=_params(2),
        name="sparse_attention",
    )(p_small, p_small, p_main, p_main, p_small, p_small, p_small, gq, gk, gik)


def _merge_kernel(ar_ref, aa_ref, wr_ref, wa_ref, ga_ref, gb_ref, m_ref):
    u_ret = jnp.dot(ar_ref[...], wr_ref[...], preferred_element_type=F32)
    u_att = jnp.dot(aa_ref[...], wa_ref[...], preferred_element_type=F32)
    m = jax.nn.sigmoid(ga_ref[...].astype(F32)) * u_ret + jax.nn.sigmoid(gb_ref[...].astype(F32)) * u_att
    m_ref[...] = m.astype(m_ref.dtype)


def _merge(a_ret, a_att, w_ret, w_att, p_main, layer):
    T, K = a_ret.shape
    D = w_ret.shape[-1]
    tm = _tile(T, 512)
    tn = _tile(D, 1024)
    gate_a = 6 * (K // tn)
    gate_b = 7 * (K // tn)
    return pl.pallas_call(
        _merge_kernel,
        grid=(D // tn, T // tm),
        in_specs=[pl.BlockSpec((tm, K), lambda j, i: (i, 0)),
                  pl.BlockSpec((tm, K), lambda j, i: (i, 0)),
                  pl.BlockSpec((None, K, tn), lambda j, i: (layer, 0, j)),
                  pl.BlockSpec((None, K, tn), lambda j, i: (layer, 0, j)),
                  pl.BlockSpec((tm, tn), lambda j, i: (i, gate_a + j)),
                  pl.BlockSpec((tm, tn), lambda j, i: (i, gate_b + j))],
        out_specs=pl.BlockSpec((tm, tn), lambda j, i: (i, j)),
        out_shape=jax.ShapeDtypeStruct((T, D), BF16),
        compiler_params=_params(2),
        name="merge",
    )(a_ret, a_att, w_ret, w_att, p_main, p_main)


def _output_kernel(m_ref, w_ref, x_ref, g_ref, xo_ref, h_ref):
    x = x_ref[...] + jnp.dot(m_ref[...], w_ref[...], preferred_element_type=F32)
    xo_ref[...] = x
    r = lax.rsqrt(jnp.mean(x * x, axis=-1, keepdims=True) + EPS)
    h_ref[...] = (x * r * g_ref[...]).astype(h_ref.dtype)


def _output_last_kernel(m_ref, w_ref, x_ref, xo_ref):
    xo_ref[...] = x_ref[...] + jnp.dot(m_ref[...], w_ref[...], preferred_element_type=F32)


def _output(m, w_out, x, norm_g, layer, last):
    T, D = x.shape
    tm = _tile(T, 256)
    row_spec = pl.BlockSpec((tm, D), lambda i: (i, 0))
    in_specs = [row_spec, pl.BlockSpec((None, D, D), lambda i: (layer, 0, 0)), row_spec]
    if last:
        return pl.pallas_call(
            _output_last_kernel, grid=(T // tm,), in_specs=in_specs, out_specs=row_spec,
            out_shape=jax.ShapeDtypeStruct((T, D), F32), compiler_params=_params(1), name="output_last",
        )(m, w_out, x), None
    return pl.pallas_call(
        _output_kernel, grid=(T // tm,),
        in_specs=in_specs + [pl.BlockSpec((None, 1, D), lambda i: (layer + 1, 0, 0))],
        out_specs=[row_spec, row_spec],
        out_shape=[jax.ShapeDtypeStruct((T, D), F32), jax.ShapeDtypeStruct((T, D), BF16)],
        compiler_params=_params(1), name="output",
    )(m, w_out, x, norm_g)


def _split_input_weights(w_in):
    D = RET_HEADS * RET_DK
    o_aq = 4 * D
    o_ak = o_aq + ATT_HEADS * ATT_DH
    o_av = o_ak + ATT_DH
    o_ag = o_av + ATT_DH
    o_iq = o_ag + ATT_HEADS * ATT_DH
    o_ik = o_iq + IDX_HEADS * IDX_DH
    o_iw = o_ik + IDX_DH
    o_ga = o_iw + IDX_HEADS
    main = jnp.concatenate([w_in[..., :o_ak], w_in[..., o_ag:o_iq], w_in[..., o_ga:]], axis=-1)
    zeros = lambda n: jnp.zeros(w_in.shape[:-1] + (n,), w_in.dtype)
    small = jnp.concatenate([w_in[..., o_iq:o_ik], w_in[..., o_ak:o_ag],
                             w_in[..., o_ik:o_iw], zeros(LANES - IDX_DH),
                             w_in[..., o_iw:o_ga], zeros(LANES - IDX_HEADS)], axis=-1)
    return main.astype(BF16), small.astype(BF16)


def kernel(x, norm_g, w_in, ret_out_g, att_q_g, att_k_g, idx_k_g, w_branch_ret, w_branch_att, w_out):
    B, S, D = x.shape
    depth = w_in.shape[0]
    T = B * S
    w_main, w_small = _split_input_weights(w_in)
    w_ret = w_branch_ret.astype(BF16)
    w_att = w_branch_att.astype(BF16)
    w_o = w_out.astype(BF16)
    norm_g3 = norm_g.reshape(depth, 1, D)
    ret_g4 = ret_out_g.reshape(depth, RET_HEADS, 1, RET_DV)
    gq = att_q_g.reshape(depth, ATT_DH, 1)
    gk = att_k_g.reshape(depth, 1, ATT_DH)
    gik = jnp.pad(idx_k_g, ((0, 0), (0, LANES - IDX_DH))).reshape(depth, 1, LANES)
    tables = _retention_tables(S, _tile(S, 256))

    xf = x.reshape(T, D)
    h = _rmsnorm(xf, norm_g3, 0)
    for layer in range(depth):
        p_main = _project(h, w_main, layer, 1024, "project_main")
        p_small = _project(h, w_small, layer, w_small.shape[-1], "project_small")
        a_ret = _retention(p_main, tables, ret_g4, layer, B, S)
        a_att = _sparse_attention(p_main, p_small, gq, gk, gik, layer, B, S)
        m = _merge(a_ret, a_att, w_ret, w_att, p_main, layer)
        xf, h = _output(m, w_o, xf, norm_g3, layer, last=layer == depth - 1)
    return xf.reshape(B, S, D)
```

```python
import functools

import jax
import jax.numpy as jnp
from jax import lax
from jax.experimental import pallas as pl
from jax.experimental.pallas import tpu as pltpu

F32 = jnp.float32
BF16 = jnp.bfloat16

CHUNK = 64
EPS = 1e-6
RET_HEADS = 8
RET_DK = 256
RET_DV = 256
ROT_BASE = 10000.0
ATT_HEADS = 16
ATT_DH = 128
IDX_HEADS = 16
IDX_DH = 64
INDEX_TOPK = 256
Q_BLOCK = 128
LANES = 128
KEY_ROWS = 256

INT_MIN = -(2 ** 31)
KEY_NEG_INF = -2139095041
KEY_POS_INF = 2139095040
MASKED_LOGIT = -1e30
LOG2_E = 1.4426950408889634

VMEM_LIMIT_BYTES = 56 * 1024 * 1024


def _params(n_axes):
    return pltpu.CompilerParams(dimension_semantics=("arbitrary",) * n_axes,
                                vmem_limit_bytes=VMEM_LIMIT_BYTES)


def _tile(n, pref):
    t = min(n, pref)
    assert n % t == 0, (n, t)
    return t


def _silu(g):
    return g * jax.nn.sigmoid(g)


def _fold_rows(x, op):
    SUBLANES = 8
    while x.shape[0] > SUBLANES and x.shape[0] % (2 * SUBLANES) == 0:
        half = x.shape[0] // 2
        x = op(x[:half], x[half:])
    return x


def _reduce_rows(x, op, finish):
    return finish(_fold_rows(x, op), axis=0, keepdims=True)


def _rmsnorm_kernel(x_ref, g_ref, h_ref):
    x = x_ref[...]
    r = lax.rsqrt(jnp.mean(x * x, axis=-1, keepdims=True) + EPS)
    h_ref[...] = (x * r * g_ref[...]).astype(h_ref.dtype)


def _rmsnorm(x, norm_g, layer):
    T, D = x.shape
    tm = _tile(T, 512)
    return pl.pallas_call(
        _rmsnorm_kernel,
        grid=(T // tm,),
        in_specs=[pl.BlockSpec((tm, D), lambda i: (i, 0)),
                  pl.BlockSpec((None, 1, D), lambda i: (layer, 0, 0))],
        out_specs=pl.BlockSpec((tm, D), lambda i: (i, 0)),
        out_shape=jax.ShapeDtypeStruct((T, D), BF16),
        compiler_params=_params(1),
        name="rmsnorm",
    )(x, norm_g)


def _matmul_kernel(a_ref, b_ref, o_ref):
    o_ref[...] = jnp.dot(a_ref[...], b_ref[...], preferred_element_type=F32).astype(o_ref.dtype)


def _project(h, w, layer, tn, name):
    T, K = h.shape
    N = w.shape[-1]
    tm = _tile(T, 1024)
    return pl.pallas_call(
        _matmul_kernel,
        grid=(N // tn, T // tm),
        in_specs=[pl.BlockSpec((tm, K), lambda j, i: (i, 0)),
                  pl.BlockSpec((None, K, tn), lambda j, i: (layer, 0, j))],
        out_specs=pl.BlockSpec((tm, tn), lambda j, i: (i, j)),
        out_shape=jax.ShapeDtypeStruct((T, N), BF16),
        compiler_params=_params(2),
        name=name,
    )(h, w)


def _retention_tables(S, R):
    pos = jnp.arange(S, dtype=F32)
    inv = 1.0 / (ROT_BASE ** jnp.linspace(0.0, 1.0, RET_DK // 2, dtype=F32))
    ang = pos[:, None] * inv[None, :]
    log_g = jnp.log(1.0 - 2.0 ** (-5.0 - jnp.arange(RET_HEADS, dtype=F32)))[:, None, None]
    idx = jnp.arange(R, dtype=F32)
    chunk = jnp.arange(R) // CHUNK
    visible = chunk[None, :] <= chunk[:, None]
    dm = jnp.where(visible[None], jnp.exp(jnp.abs(idx[:, None] - idx[None, :])[None] * log_g), 0.0)
    ones = jnp.ones((1, 1, RET_DV), F32)
    qd = jnp.exp((idx[None, :, None] + 1.0) * log_g) * ones
    kd = jnp.exp((R - 1.0 - idx)[None, :, None] * log_g) * ones
    cd = jnp.exp(float(R) * log_g) * ones
    return jnp.cos(ang), jnp.sin(ang), dm, qd, kd, cd


def _retention_kernel(q_ref, k_ref, v_ref, g_ref, cos_ref, sin_ref, dm_ref, qd_ref, kd_ref, cd_ref,
                      og_ref, o_ref, state_ref):
    @pl.when(pl.program_id(1) == 0)
    def _():
        state_ref[...] = jnp.zeros_like(state_ref)

    cos = cos_ref[...]
    sin = sin_ref[...]
    half = RET_DK // 2

    def rotated(ref, h):
        x1 = ref[:, h * RET_DK:h * RET_DK + half].astype(F32)
        x2 = ref[:, h * RET_DK + half:(h + 1) * RET_DK].astype(F32)
        return jnp.concatenate([x1 * cos - x2 * sin, x2 * cos + x1 * sin], axis=1)

    for h in range(RET_HEADS):
        cols = slice(h * RET_DV, (h + 1) * RET_DV)
        qr = rotated(q_ref, h)
        kr = rotated(k_ref, h) * (RET_DK ** -0.5)
        qb = qr.astype(BF16)
        kb = kr.astype(BF16)
        v = v_ref[:, cols]
        s = lax.dot_general(qb, kb, (((1,), (1,)), ((), ())), preferred_element_type=F32) * dm_ref[h]
        o = jnp.dot(s.astype(BF16), v, preferred_element_type=F32)
        state = state_ref[h]
        o = o + jnp.dot(qb, state.astype(BF16), preferred_element_type=F32) * qd_ref[h]
        kdec = (kr * kd_ref[h]).astype(BF16)
        state_ref[h] = state * cd_ref[h] + lax.dot_general(
            kdec, v, (((0,), (0,)), ((), ())), preferred_element_type=F32)
        r = lax.rsqrt(jnp.mean(o * o, axis=-1, keepdims=True) + EPS)
        g = g_ref[:, cols].astype(F32)
        o_ref[:, cols] = (o * r * og_ref[h] * _silu(g)).astype(o_ref.dtype)


def _retention(p_main, tables, ret_out_g, layer, B, S):
    T = p_main.shape[0]
    W = RET_HEADS * RET_DV
    R = tables[2].shape[1]
    nr = S // R
    cos, sin, dm, qd, kd, cd = tables
    row = lambda b, i: b * nr + i
    full3 = lambda a: pl.BlockSpec(a.shape, lambda b, i: (0, 0, 0))
    return pl.pallas_call(
        _retention_kernel,
        grid=(B, nr),
        in_specs=[pl.BlockSpec((R, W), lambda b, i: (row(b, i), 0)),
                  pl.BlockSpec((R, W), lambda b, i: (row(b, i), 1)),
                  pl.BlockSpec((R, W), lambda b, i: (row(b, i), 2)),
                  pl.BlockSpec((R, W), lambda b, i: (row(b, i), 3)),
                  pl.BlockSpec((R, RET_DK // 2), lambda b, i: (i, 0)),
                  pl.BlockSpec((R, RET_DK // 2), lambda b, i: (i, 0)),
                  full3(dm), full3(qd), full3(kd), full3(cd),
                  pl.BlockSpec((None, RET_HEADS, 1, RET_DV), lambda b, i: (layer, 0, 0, 0))],
        out_specs=pl.BlockSpec((R, W), lambda b, i: (row(b, i), 0)),
        out_shape=jax.ShapeDtypeStruct((T, W), BF16),
        scratch_shapes=[pltpu.VMEM((RET_HEADS, RET_DK, RET_DV), F32)],
        compiler_params=_params(2),
        name="retention",
    )(p_main, p_main, p_main, p_main, cos, sin, dm, qd, kd, cd, ret_out_g)


def _prepare_keys_kernel(ak_ref, av_ref, ik_ref, gk_ref, gik_ref, kn_ref, vt_ref, ikn_ref):
    ak = ak_ref[...].astype(F32)
    r = lax.rsqrt(jnp.mean(ak * ak, axis=-1, keepdims=True) + EPS)
    kn_ref[...] = (ak * r * gk_ref[...]).astype(BF16)
    vt_ref[...] = av_ref[...].astype(F32).T.astype(BF16)
    ik = ik_ref[...].astype(F32)
    r = lax.rsqrt(jnp.sum(ik * ik, axis=-1, keepdims=True) * (1.0 / IDX_DH) + EPS)
    ikn = ik * r * gik_ref[...]
    ikn_ref[0] = ikn.astype(BF16)
    ikn_ref[1] = pltpu.roll(ikn, IDX_DH, 1).astype(BF16)


def _prepare_keys(p_small, gk, gik, layer, B, S):
    seg = IDX_HEADS * IDX_DH // LANES
    return pl.pallas_call(
        _prepare_keys_kernel,
        grid=(B,),
        in_specs=[pl.BlockSpec((S, LANES), lambda b: (b, seg)),
                  pl.BlockSpec((S, LANES), lambda b: (b, seg + 1)),
                  pl.BlockSpec((S, LANES), lambda b: (b, seg + 2)),
                  pl.BlockSpec((None, 1, ATT_DH), lambda b: (layer, 0, 0)),
                  pl.BlockSpec((None, 1, LANES), lambda b: (layer, 0, 0))],
        out_specs=[pl.BlockSpec((None, S, ATT_DH), lambda b: (b, 0, 0)),
                   pl.BlockSpec((None, ATT_DH, S), lambda b: (b, 0, 0)),
                   pl.BlockSpec((None, 2, S, LANES), lambda b: (b, 0, 0, 0))],
        out_shape=[jax.ShapeDtypeStruct((B, S, ATT_DH), BF16),
                   jax.ShapeDtypeStruct((B, ATT_DH, S), BF16),
                   jax.ShapeDtypeStruct((B, 2, S, LANES), BF16)],
        compiler_params=_params(1),
        name="prepare_keys",
    )(p_small, p_small, p_small, gk, gik)


def _sparse_attention_kernel(iq_ref, iw_ref, aq_ref, ag_ref, kn_ref, vt_ref, ikn_ref, gq_ref, *rest,
                             topk, first_block):
    o_ref, key_ref, bias_ref, qt_ref, s_ref, p_ref, ot_ref = rest[-7:]
    S = kn_ref.shape[0]
    TQ = iq_ref.shape[0]
    i = pl.program_id(1) + first_block

    iqT = iq_ref[...].astype(F32).T.astype(BF16)
    wT = iw_ref[...].astype(F32).T * (IDX_HEADS ** -0.5 * IDX_DH ** -0.5)
    score = jnp.zeros((S, TQ), F32)
    for hp in range(IDX_HEADS // 2):
        slab = iqT[hp * LANES:(hp + 1) * LANES, :]
        s0 = jnp.dot(ikn_ref[0], slab, preferred_element_type=F32)
        s1 = jnp.dot(ikn_ref[1], slab, preferred_element_type=F32)
        score = score + jnp.maximum(s0, 0.0) * wT[2 * hp:2 * hp + 1, :]
        score = score + jnp.maximum(s1, 0.0) * wT[2 * hp + 1:2 * hp + 2, :]

    bits = lax.bitcast_convert_type(score, jnp.int32)
    key = bits ^ ((bits >> 31) & 0x7FFFFFFF)
    kpos = lax.broadcasted_iota(jnp.int32, (S, TQ), 0)
    qpos = lax.broadcasted_iota(jnp.int32, (S, TQ), 1) + i * TQ
    key = jnp.where(kpos // CHUNK <= qpos // CHUNK, key, INT_MIN)
    key_ref[...] = key

    def count(pred):
        acc = jnp.zeros((KEY_ROWS, TQ), F32)
        for c in range(S // KEY_ROWS):
            rows = slice(c * KEY_ROWS, (c + 1) * KEY_ROWS)
            kpos = lax.broadcasted_iota(jnp.int32, (KEY_ROWS, TQ), 0) + c * KEY_ROWS
            acc = acc + jnp.where(pred(key_ref[rows, :], kpos), 1.0, 0.0)
        return _reduce_rows(acc, jnp.add, jnp.sum)

    def try_threshold(cand, thr, cnt):
        c = count(lambda key, kpos: key >= cand)
        ok = c >= topk
        return jnp.where(ok, cand, thr), jnp.where(ok, c, cnt)

    thr = jnp.full((1, TQ), INT_MIN, jnp.int32)
    cnt = jnp.full((1, TQ), float(S), F32)
    thr, cnt = try_threshold(jnp.zeros((1, TQ), jnp.int32), thr, cnt)

    def bit_step(it, carry):
        thr, cnt = carry
        return try_threshold(thr | (1 << (30 - it)), thr, cnt)

    thr, cnt = lax.fori_loop(0, 31, bit_step, (thr, cnt))

    def selected_bias(c, selected):
        key = key_ref[c * KEY_ROWS:(c + 1) * KEY_ROWS, :]
        kpos = lax.broadcasted_iota(jnp.int32, (KEY_ROWS, TQ), 0) + c * KEY_ROWS
        finite = (key > KEY_NEG_INF) & (key < KEY_POS_INF)
        b = jnp.where(selected(key, kpos) & finite, 0.0, MASKED_LOGIT)
        bias_ref[c * KEY_ROWS:(c + 1) * KEY_ROWS, :] = jnp.concatenate([b, b], axis=1)

    for c in range(S // KEY_ROWS):
        selected_bias(c, lambda key, kpos: key >= thr)

    tied = jnp.where((cnt > topk) & (thr > KEY_NEG_INF), 1.0, 0.0)

    @pl.when(jnp.max(tied) > 0.0)
    def _break_ties():
        need = topk - count(lambda key, kpos: key > thr)

        def pos_step(it, bound):
            cand = bound | (1 << (S.bit_length() - 1 - it))
            ok = count(lambda key, kpos: (key == thr) & (kpos < cand)) <= need
            return jnp.where(ok, cand, bound)

        bound = lax.fori_loop(0, S.bit_length(), pos_step, jnp.zeros((1, TQ), jnp.int32))
        for c in range(S // KEY_ROWS):
            selected_bias(c, lambda key, kpos: (key > thr) | ((key == thr) & (kpos < bound)))

    aqT = aq_ref[...].astype(F32).T
    gq = gq_ref[...] * (ATT_DH ** -0.5 * LOG2_E)
    for h in range(ATT_HEADS):
        x = aqT[h * ATT_DH:(h + 1) * ATT_DH, :]
        r = lax.rsqrt(_reduce_rows(x * x, jnp.add, jnp.sum) * (1.0 / ATT_DH) + EPS)
        qt_ref[h * ATT_DH:(h + 1) * ATT_DH, :] = (x * r * gq).astype(BF16)

    for hp in range(ATT_HEADS // 2):
        row0 = hp * 2 * ATT_DH
        q_pair = jnp.concatenate([qt_ref[pl.ds(row0, ATT_DH), :], qt_ref[pl.ds(row0 + ATT_DH, ATT_DH), :]], axis=1)
        slot = hp % 2
        top = jnp.full((8, 2 * TQ), -jnp.inf, F32)
        for c in range(S // KEY_ROWS):
            rows = slice(c * KEY_ROWS, (c + 1) * KEY_ROWS)
            s = jnp.dot(kn_ref[rows, :], q_pair, preferred_element_type=F32) + bias_ref[rows, :]
            s_ref[slot, rows, :] = s
            top = jnp.maximum(top, _fold_rows(s, jnp.maximum))
        top = jnp.max(top, axis=0, keepdims=True)
        den = jnp.zeros((8, 2 * TQ), F32)
        for c in range(S // KEY_ROWS):
            rows = slice(c * KEY_ROWS, (c + 1) * KEY_ROWS)
            p = jnp.exp2(s_ref[slot, rows, :] - top)
            den = den + _fold_rows(p, jnp.add)
            p_ref[slot, rows, :] = p.astype(BF16)
        den = jnp.sum(den, axis=0, keepdims=True)
        oT = jnp.dot(vt_ref[...], p_ref[slot], preferred_element_type=F32) / den
        ot_ref[pl.ds(row0, ATT_DH), :] = oT[:, :TQ]
        ot_ref[pl.ds(row0 + ATT_DH, ATT_DH), :] = oT[:, TQ:]

    o_ref[...] = (ot_ref[...].T * _silu(ag_ref[...].astype(F32))).astype(o_ref.dtype)


def _sparse_attention(p_main, p_small, keys, gq, layer, B, S):
    T = p_main.shape[0]
    W = ATT_HEADS * ATT_DH
    TQ = Q_BLOCK
    nb = S // TQ
    topk = min(INDEX_TOPK, S // 4)
    n_iq = IDX_HEADS * IDX_DH
    seg = n_iq // LANES
    kn, vt, ikn = keys
    group_keys = max(KEY_ROWS, S // 4)
    group_blocks = group_keys // TQ
    out = None
    for first_block in range(0, nb, group_blocks):
        n_keys = (first_block + group_blocks) * TQ
        qrow = lambda b, i, first_block=first_block: b * nb + first_block + i
        in_specs = [pl.BlockSpec((TQ, n_iq), lambda b, i: (qrow(b, i), 0)),
                    pl.BlockSpec((TQ, LANES), lambda b, i: (qrow(b, i), seg + 3)),
                    pl.BlockSpec((TQ, W), lambda b, i: (qrow(b, i), 4)),
                    pl.BlockSpec((TQ, W), lambda b, i: (qrow(b, i), 5)),
                    pl.BlockSpec((None, n_keys, ATT_DH), lambda b, i: (b, 0, 0)),
                    pl.BlockSpec((None, ATT_DH, n_keys), lambda b, i: (b, 0, 0)),
                    pl.BlockSpec((None, 2, n_keys, LANES), lambda b, i: (b, 0, 0, 0)),
                    pl.BlockSpec((None, ATT_DH, 1), lambda b, i: (layer, 0, 0))]
        args = [p_small, p_small, p_main, p_main, kn, vt, ikn, gq]
        aliases = {}
        if out is not None:
            in_specs.append(pl.BlockSpec(memory_space=pl.ANY))
            aliases = {len(args): 0}
            args.append(out)
        out = pl.pallas_call(
            functools.partial(_sparse_attention_kernel, topk=topk, first_block=first_block),
            grid=(B, group_blocks),
            in_specs=in_specs,
            out_specs=pl.BlockSpec((TQ, W), lambda b, i: (qrow(b, i), 0)),
            out_shape=jax.ShapeDtypeStruct((T, W), BF16),
            scratch_shapes=[pltpu.VMEM((n_keys, TQ), jnp.int32),
                            pltpu.VMEM((n_keys, 2 * TQ), F32),
                            pltpu.VMEM((W, TQ), BF16),
                            pltpu.VMEM((2, n_keys, 2 * TQ), F32),
                            pltpu.VMEM((2, n_keys, 2 * TQ), BF16),
                            pltpu.VMEM((W, TQ), F32)],
            input_output_aliases=aliases,
            compiler_params=_params(2),
            name="sparse_attention",
        )(*args)
    return out


def _merge_kernel(ar_ref, aa_ref, wr_ref, wa_ref, ga_ref, gb_ref, m_ref):
    u_ret = jnp.dot(ar_ref[...], wr_ref[...], preferred_element_type=F32)
    u_att = jnp.dot(aa_ref[...], wa_ref[...], preferred_element_type=F32)
    m = jax.nn.sigmoid(ga_ref[...].astype(F32)) * u_ret + jax.nn.sigmoid(gb_ref[...].astype(F32)) * u_att
    m_ref[...] = m.astype(m_ref.dtype)


def _merge(a_ret, a_att, w_ret, w_att, p_main, layer):
    T, K = a_ret.shape
    D = w_ret.shape[-1]
    tm = _tile(T, 512)
    tn = _tile(D, 1024)
    gate_a = 6 * (K // tn)
    gate_b = 7 * (K // tn)
    return pl.pallas_call(
        _merge_kernel,
        grid=(D // tn, T // tm),
        in_specs=[pl.BlockSpec((tm, K), lambda j, i: (i, 0)),
                  pl.BlockSpec((tm, K), lambda j, i: (i, 0)),
                  pl.BlockSpec((None, K, tn), lambda j, i: (layer, 0, j)),
                  pl.BlockSpec((None, K, tn), lambda j, i: (layer, 0, j)),
                  pl.BlockSpec((tm, tn), lambda j, i: (i, gate_a + j)),
                  pl.BlockSpec((tm, tn), lambda j, i: (i, gate_b + j))],
        out_specs=pl.BlockSpec((tm, tn), lambda j, i: (i, j)),
        out_shape=jax.ShapeDtypeStruct((T, D), BF16),
        compiler_params=_params(2),
        name="merge",
    )(a_ret, a_att, w_ret, w_att, p_main, p_main)


def _output_kernel(m_ref, w_ref, x_ref, g_ref, xo_ref, h_ref):
    x = x_ref[...] + jnp.dot(m_ref[...], w_ref[...], preferred_element_type=F32)
    xo_ref[...] = x
    r = lax.rsqrt(jnp.mean(x * x, axis=-1, keepdims=True) + EPS)
    h_ref[...] = (x * r * g_ref[...]).astype(h_ref.dtype)


def _output_last_kernel(m_ref, w_ref, x_ref, xo_ref):
    xo_ref[...] = x_ref[...] + jnp.dot(m_ref[...], w_ref[...], preferred_element_type=F32)


def _output(m, w_out, x, norm_g, layer, last):
    T, D = x.shape
    tm = _tile(T, 256)
    row_spec = pl.BlockSpec((tm, D), lambda i: (i, 0))
    in_specs = [row_spec, pl.BlockSpec((None, D, D), lambda i: (layer, 0, 0)), row_spec]
    if last:
        return pl.pallas_call(
            _output_last_kernel, grid=(T // tm,), in_specs=in_specs, out_specs=row_spec,
            out_shape=jax.ShapeDtypeStruct((T, D), F32), compiler_params=_params(1), name="output_last",
        )(m, w_out, x), None
    return pl.pallas_call(
        _output_kernel, grid=(T // tm,),
        in_specs=in_specs + [pl.BlockSpec((None, 1, D), lambda i: (layer + 1, 0, 0))],
        out_specs=[row_spec, row_spec],
        out_shape=[jax.ShapeDtypeStruct((T, D), F32), jax.ShapeDtypeStruct((T, D), BF16)],
        compiler_params=_params(1), name="output",
    )(m, w_out, x, norm_g)


def _split_input_weights(w_in):
    D = RET_HEADS * RET_DK
    o_aq = 4 * D
    o_ak = o_aq + ATT_HEADS * ATT_DH
    o_av = o_ak + ATT_DH
    o_ag = o_av + ATT_DH
    o_iq = o_ag + ATT_HEADS * ATT_DH
    o_ik = o_iq + IDX_HEADS * IDX_DH
    o_iw = o_ik + IDX_DH
    o_ga = o_iw + IDX_HEADS
    main = jnp.concatenate([w_in[..., :o_ak], w_in[..., o_ag:o_iq], w_in[..., o_ga:]], axis=-1)
    zeros = lambda n: jnp.zeros(w_in.shape[:-1] + (n,), w_in.dtype)
    small = jnp.concatenate([w_in[..., o_iq:o_ik], w_in[..., o_ak:o_ag],
                             w_in[..., o_ik:o_iw], zeros(LANES - IDX_DH),
                             w_in[..., o_iw:o_ga], zeros(LANES - IDX_HEADS)], axis=-1)
    return main.astype(BF16), small.astype(BF16)


def kernel(x, norm_g, w_in, ret_out_g, att_q_g, att_k_g, idx_k_g, w_branch_ret, w_branch_att, w_out):
    B, S, D = x.shape
    depth = w_in.shape[0]
    T = B * S
    w_main, w_small = _split_input_weights(w_in)
    w_ret = w_branch_ret.astype(BF16)
    w_att = w_branch_att.astype(BF16)
    w_o = w_out.astype(BF16)
    norm_g3 = norm_g.reshape(depth, 1, D)
    ret_g4 = ret_out_g.reshape(depth, RET_HEADS, 1, RET_DV)
    gq = att_q_g.reshape(depth, ATT_DH, 1)
    gk = att_k_g.reshape(depth, 1, ATT_DH)
    gik = jnp.pad(idx_k_g, ((0, 0), (0, LANES - IDX_DH))).reshape(depth, 1, LANES)
    tables = _retention_tables(S, _tile(S, 256))

    xf = x.reshape(T, D)
    h = _rmsnorm(xf, norm_g3, 0)
    for layer in range(depth):
        p_main = _project(h, w_main, layer, 1024, "project_main")
        p_small = _project(h, w_small, layer, w_small.shape[-1], "project_small")
        a_ret = _retention(p_main, tables, ret_g4, layer, B, S)
        keys = _prepare_keys(p_small, gk, gik, layer, B, S)
        a_att = _sparse_attention(p_main, p_small, keys, gq, layer, B, S)
        m = _merge(a_ret, a_att, w_ret, w_att, p_main, layer)
        xf, h = _output(m, w_o, xf, norm_g3, layer, last=layer == depth - 1)
    return xf.reshape(B, S, D)
```

```python
import functools

import jax
import jax.numpy as jnp
from jax import lax
from jax.experimental import pallas as pl
from jax.experimental.pallas import tpu as pltpu

F32 = jnp.float32
BF16 = jnp.bfloat16

CHUNK = 64
EPS = 1e-6
RET_HEADS = 8
RET_DK = 256
RET_DV = 256
ROT_BASE = 10000.0
ATT_HEADS = 16
ATT_DH = 128
IDX_HEADS = 16
IDX_DH = 64
INDEX_TOPK = 256
Q_BLOCK = 128
LANES = 128
KEY_ROWS = 256

INT_MIN = -(2 ** 31)
KEY_NEG_INF = -2139095041
KEY_POS_INF = 2139095040
MASKED_LOGIT = -1e30
LOG2_E = 1.4426950408889634

VMEM_LIMIT_BYTES = 56 * 1024 * 1024


def _params(n_axes):
    return pltpu.CompilerParams(dimension_semantics=("arbitrary",) * n_axes,
                                vmem_limit_bytes=VMEM_LIMIT_BYTES)


def _tile(n, pref):
    t = min(n, pref)
    assert n % t == 0, (n, t)
    return t


def _silu(g):
    return g * jax.nn.sigmoid(g)


def _fold_rows(x, op):
    SUBLANES = 8
    while x.shape[0] > SUBLANES and x.shape[0] % (2 * SUBLANES) == 0:
        half = x.shape[0] // 2
        x = op(x[:half], x[half:])
    return x


def _reduce_rows(x, op, finish):
    return finish(_fold_rows(x, op), axis=0, keepdims=True)


def _rmsnorm_kernel(x_ref, g_ref, h_ref):
    x = x_ref[...]
    r = lax.rsqrt(jnp.mean(x * x, axis=-1, keepdims=True) + EPS)
    h_ref[...] = (x * r * g_ref[...]).astype(h_ref.dtype)


def _rmsnorm(x, norm_g, layer):
    T, D = x.shape
    tm = _tile(T, 512)
    return pl.pallas_call(
        _rmsnorm_kernel,
        grid=(T // tm,),
        in_specs=[pl.BlockSpec((tm, D), lambda i: (i, 0)),
                  pl.BlockSpec((None, 1, D), lambda i: (layer, 0, 0))],
        out_specs=pl.BlockSpec((tm, D), lambda i: (i, 0)),
        out_shape=jax.ShapeDtypeStruct((T, D), BF16),
        compiler_params=_params(1),
        name="rmsnorm",
    )(x, norm_g)


OFF_AQ = 4 * RET_HEADS * RET_DK
OFF_AK = OFF_AQ + ATT_HEADS * ATT_DH
OFF_AV = OFF_AK + ATT_DH
OFF_AG = OFF_AV + ATT_DH
OFF_IQ = OFF_AG + ATT_HEADS * ATT_DH
OFF_IK = OFF_IQ + IDX_HEADS * IDX_DH
OFF_IW = OFF_IK + IDX_DH
OFF_GA = OFF_IW + IDX_HEADS
N_IN = OFF_GA + 2 * RET_HEADS * RET_DK


def _project_kernel(a_ref, *refs, shift, tn):
    *w_refs, o_ref, wb_ref = refs

    @pl.when(pl.program_id(1) == 0)
    def _():
        parts = [r[0] if len(r.shape) == 3 else r[...] for r in w_refs]
        w = parts[0] if len(parts) == 1 else jnp.concatenate(parts, axis=1)
        wb_ref[...] = w[:, shift:shift + tn].astype(BF16)

    o_ref[...] = jnp.dot(a_ref[...], wb_ref[...], preferred_element_type=F32).astype(o_ref.dtype)


def _project(h, w_in, w_specs, n_tiles, tn, shift, name):
    T, K = h.shape
    tm = _tile(T, 1024)
    return pl.pallas_call(
        functools.partial(_project_kernel, shift=shift, tn=tn),
        grid=(n_tiles, T // tm),
        in_specs=[pl.BlockSpec((tm, K), lambda j, i: (i, 0))] + w_specs,
        out_specs=pl.BlockSpec((tm, tn), lambda j, i: (i, j)),
        out_shape=jax.ShapeDtypeStruct((T, n_tiles * tn), BF16),
        scratch_shapes=[pltpu.VMEM((K, tn), BF16)],
        compiler_params=_params(2),
        name=name,
    )(h, *([w_in] * len(w_specs)))


def _project_main(h, w_in, layer):
    K = h.shape[1]
    tn = 1024
    first = OFF_AK // tn
    gap = (OFF_AG - OFF_AK) // LANES
    assert OFF_AK % tn == 0 and (OFF_AG - OFF_AK) % LANES == 0 and (OFF_IQ - OFF_AG) % tn == 0
    n_tiles = first + (OFF_IQ - OFF_AG) // tn
    spec = pl.BlockSpec(
        (pl.Element(1), pl.Element(K), pl.Element(tn)),
        lambda j, i: (layer, 0, (j * (tn // LANES) + jnp.where(j >= first, gap, 0)) * LANES))
    return _project(h, w_in, [spec], n_tiles, tn, 0, "project_main")


def _project_gates(h, w_in, layer):
    K = h.shape[1]
    tn = 512
    base, shift = divmod(OFF_GA, LANES)
    n_blocks = tn // LANES + 1
    specs = [pl.BlockSpec((None, K, LANES), lambda j, i, r=r: (layer, 0, base + j * (tn // LANES) + r))
             for r in range(n_blocks)]
    return _project(h, w_in, specs, (N_IN - OFF_GA) // tn, tn, shift, "project_gates")


def _project_small(h, w_in, layer):
    K = h.shape[1]
    n_iq = IDX_HEADS * IDX_DH
    wide = 2 * LANES
    assert OFF_IQ % wide == 0 and n_iq % wide == 0 and OFF_AK % LANES == 0 and OFF_IK % LANES == 0
    specs = [pl.BlockSpec((None, K, wide), lambda j, i, r=r: (layer, 0, OFF_IQ // wide + r))
             for r in range(n_iq // wide)]
    specs += [pl.BlockSpec((None, K, LANES), lambda j, i, c=c: (layer, 0, c // LANES))
              for c in (OFF_AK, OFF_AV, OFF_IK)]
    return _project(h, w_in, specs, 1, n_iq + 3 * LANES, 0, "project_small")


def _retention_tables(S, R):
    pos = jnp.arange(S, dtype=F32)
    inv = 1.0 / (ROT_BASE ** jnp.linspace(0.0, 1.0, RET_DK // 2, dtype=F32))
    ang = pos[:, None] * inv[None, :]
    log_g = jnp.log(1.0 - 2.0 ** (-5.0 - jnp.arange(RET_HEADS, dtype=F32)))[:, None, None]
    idx = jnp.arange(R, dtype=F32)
    chunk = jnp.arange(R) // CHUNK
    visible = chunk[None, :] <= chunk[:, None]
    dm = jnp.where(visible[None], jnp.exp(jnp.abs(idx[:, None] - idx[None, :])[None] * log_g), 0.0)
    ones = jnp.ones((1, 1, RET_DV), F32)
    qd = jnp.exp((idx[None, :, None] + 1.0) * log_g) * ones
    kd = jnp.exp((R - 1.0 - idx)[None, :, None] * log_g) * ones
    cd = jnp.exp(float(R) * log_g) * ones
    return jnp.cos(ang), jnp.sin(ang), dm, qd, kd, cd


def _retention_kernel(q_ref, k_ref, v_ref, g_ref, cos_ref, sin_ref, dm_ref, qd_ref, kd_ref, cd_ref,
                      og_ref, o_ref, state_ref):
    @pl.when(pl.program_id(1) == 0)
    def _():
        state_ref[...] = jnp.zeros_like(state_ref)

    cos = cos_ref[...]
    sin = sin_ref[...]
    half = RET_DK // 2

    def rotated(ref, h):
        x1 = ref[:, h * RET_DK:h * RET_DK + half].astype(F32)
        x2 = ref[:, h * RET_DK + half:(h + 1) * RET_DK].astype(F32)
        return jnp.concatenate([x1 * cos - x2 * sin, x2 * cos + x1 * sin], axis=1)

    for h in range(RET_HEADS):
        cols = slice(h * RET_DV, (h + 1) * RET_DV)
        qr = rotated(q_ref, h)
        kr = rotated(k_ref, h) * (RET_DK ** -0.5)
        qb = qr.astype(BF16)
        kb = kr.astype(BF16)
        v = v_ref[:, cols]
        s = lax.dot_general(qb, kb, (((1,), (1,)), ((), ())), preferred_element_type=F32) * dm_ref[h]
        o = jnp.dot(s.astype(BF16), v, preferred_element_type=F32)
        state = state_ref[h]
        o = o + jnp.dot(qb, state.astype(BF16), preferred_element_type=F32) * qd_ref[h]
        kdec = (kr * kd_ref[h]).astype(BF16)
        state_ref[h] = state * cd_ref[h] + lax.dot_general(
            kdec, v, (((0,), (0,)), ((), ())), preferred_element_type=F32)
        r = lax.rsqrt(jnp.mean(o * o, axis=-1, keepdims=True) + EPS)
        g = g_ref[:, cols].astype(F32)
        o_ref[:, cols] = (o * r * og_ref[h] * _silu(g)).astype(o_ref.dtype)


def _retention(p_main, tables, ret_out_g, layer, B, S):
    T = p_main.shape[0]
    W = RET_HEADS * RET_DV
    R = tables[2].shape[1]
    nr = S // R
    cos, sin, dm, qd, kd, cd = tables
    row = lambda b, i: b * nr + i
    full3 = lambda a: pl.BlockSpec(a.shape, lambda b, i: (0, 0, 0))
    return pl.pallas_call(
        _retention_kernel,
        grid=(B, nr),
        in_specs=[pl.BlockSpec((R, W), lambda b, i: (row(b, i), 0)),
                  pl.BlockSpec((R, W), lambda b, i: (row(b, i), 1)),
                  pl.BlockSpec((R, W), lambda b, i: (row(b, i), 2)),
                  pl.BlockSpec((R, W), lambda b, i: (row(b, i), 3)),
                  pl.BlockSpec((R, RET_DK // 2), lambda b, i: (i, 0)),
                  pl.BlockSpec((R, RET_DK // 2), lambda b, i: (i, 0)),
                  full3(dm), full3(qd), full3(kd), full3(cd),
                  pl.BlockSpec((None, RET_HEADS, 1, RET_DV), lambda b, i: (layer, 0, 0, 0))],
        out_specs=pl.BlockSpec((R, W), lambda b, i: (row(b, i), 0)),
        out_shape=jax.ShapeDtypeStruct((T, W), BF16),
        scratch_shapes=[pltpu.VMEM((RET_HEADS, RET_DK, RET_DV), F32)],
        compiler_params=_params(2),
        name="retention",
    )(p_main, p_main, p_main, p_main, cos, sin, dm, qd, kd, cd, ret_out_g)


def _prepare_keys_kernel(ak_ref, av_ref, ik_ref, gk_ref, gik_ref, kn_ref, vt_ref, ikn_ref):
    ak = ak_ref[...].astype(F32)
    r = lax.rsqrt(jnp.mean(ak * ak, axis=-1, keepdims=True) + EPS)
    kn_ref[...] = (ak * r * gk_ref[...]).astype(BF16)
    vt_ref[...] = av_ref[...].astype(F32).T.astype(BF16)
    ik = ik_ref[...].astype(F32)
    ik = jnp.where(lax.broadcasted_iota(jnp.int32, ik.shape, 1) < IDX_DH, ik, 0.0)
    r = lax.rsqrt(jnp.sum(ik * ik, axis=-1, keepdims=True) * (1.0 / IDX_DH) + EPS)
    ikn = ik * r * gik_ref[...]
    ikn_ref[0] = ikn.astype(BF16)
    ikn_ref[1] = pltpu.roll(ikn, IDX_DH, 1).astype(BF16)


def _prepare_keys(p_small, gk, gik, layer, B, S):
    seg = IDX_HEADS * IDX_DH // LANES
    return pl.pallas_call(
        _prepare_keys_kernel,
        grid=(B,),
        in_specs=[pl.BlockSpec((S, LANES), lambda b: (b, seg)),
                  pl.BlockSpec((S, LANES), lambda b: (b, seg + 1)),
                  pl.BlockSpec((S, LANES), lambda b: (b, seg + 2)),
                  pl.BlockSpec((None, 1, ATT_DH), lambda b: (layer, 0, 0)),
                  pl.BlockSpec((None, 1, LANES), lambda b: (layer, 0, 0))],
        out_specs=[pl.BlockSpec((None, S, ATT_DH), lambda b: (b, 0, 0)),
                   pl.BlockSpec((None, ATT_DH, S), lambda b: (b, 0, 0)),
                   pl.BlockSpec((None, 2, S, LANES), lambda b: (b, 0, 0, 0))],
        out_shape=[jax.ShapeDtypeStruct((B, S, ATT_DH), BF16),
                   jax.ShapeDtypeStruct((B, ATT_DH, S), BF16),
                   jax.ShapeDtypeStruct((B, 2, S, LANES), BF16)],
        compiler_params=_params(1),
        name="prepare_keys",
    )(p_small, p_small, p_small, gk, gik)


def _sparse_attention_kernel(iq_ref, iw_ref, aq_ref, ag_ref, kn_ref, vt_ref, ikn_ref, gq_ref, *rest,
                             topk, first_block):
    o_ref, key_ref, bias_ref, qt_ref, s_ref, p_ref, ot_ref = rest[-7:]
    S = kn_ref.shape[0]
    TQ = iq_ref.shape[0]
    i = pl.program_id(1) + first_block

    iqT = iq_ref[...].astype(F32).T.astype(BF16)
    wT = iw_ref[...].astype(F32).T[IDX_DH:IDX_DH + IDX_HEADS, :] * (IDX_HEADS ** -0.5 * IDX_DH ** -0.5)
    score = jnp.zeros((S, TQ), F32)
    for hp in range(IDX_HEADS // 2):
        slab = iqT[hp * LANES:(hp + 1) * LANES, :]
        s0 = jnp.dot(ikn_ref[0], slab, preferred_element_type=F32)
        s1 = jnp.dot(ikn_ref[1], slab, preferred_element_type=F32)
        score = score + jnp.maximum(s0, 0.0) * wT[2 * hp:2 * hp + 1, :]
        score = score + jnp.maximum(s1, 0.0) * wT[2 * hp + 1:2 * hp + 2, :]

    bits = lax.bitcast_convert_type(score, jnp.int32)
    key = bits ^ ((bits >> 31) & 0x7FFFFFFF)
    kpos = lax.broadcasted_iota(jnp.int32, (S, TQ), 0)
    qpos = lax.broadcasted_iota(jnp.int32, (S, TQ), 1) + i * TQ
    key = jnp.where(kpos // CHUNK <= qpos // CHUNK, key, INT_MIN)
    key_ref[...] = key

    def count(pred):
        acc = jnp.zeros((KEY_ROWS, TQ), F32)
        for c in range(S // KEY_ROWS):
            rows = slice(c * KEY_ROWS, (c + 1) * KEY_ROWS)
            kpos = lax.broadcasted_iota(jnp.int32, (KEY_ROWS, TQ), 0) + c * KEY_ROWS
            acc = acc + jnp.where(pred(key_ref[rows, :], kpos), 1.0, 0.0)
        return _reduce_rows(acc, jnp.add, jnp.sum)

    def try_threshold(cand, thr, cnt):
        c = count(lambda key, kpos: key >= cand)
        ok = c >= topk
        return jnp.where(ok, cand, thr), jnp.where(ok, c, cnt)

    thr = jnp.full((1, TQ), INT_MIN, jnp.int32)
    cnt = jnp.full((1, TQ), float(S), F32)
    thr, cnt = try_threshold(jnp.zeros((1, TQ), jnp.int32), thr, cnt)

    def bit_step(it, carry):
        thr, cnt = carry
        return try_threshold(thr | (1 << (30 - it)), thr, cnt)

    thr, cnt = lax.fori_loop(0, 31, bit_step, (thr, cnt))

    def selected_bias(c, selected):
        key = key_ref[c * KEY_ROWS:(c + 1) * KEY_ROWS, :]
        kpos = lax.broadcasted_iota(jnp.int32, (KEY_ROWS, TQ), 0) + c * KEY_ROWS
        finite = (key > KEY_NEG_INF) & (key < KEY_POS_INF)
        b = jnp.where(selected(key, kpos) & finite, 0.0, MASKED_LOGIT)
        bias_ref[c * KEY_ROWS:(c + 1) * KEY_ROWS, :] = jnp.concatenate([b, b], axis=1)

    for c in range(S // KEY_ROWS):
        selected_bias(c, lambda key, kpos: key >= thr)

    tied = jnp.where((cnt > topk) & (thr > KEY_NEG_INF), 1.0, 0.0)

    @pl.when(jnp.max(tied) > 0.0)
    def _break_ties():
        need = topk - count(lambda key, kpos: key > thr)

        def pos_step(it, bound):
            cand = bound | (1 << (S.bit_length() - 1 - it))
            ok = count(lambda key, kpos: (key == thr) & (kpos < cand)) <= need
            return jnp.where(ok, cand, bound)

        bound = lax.fori_loop(0, S.bit_length(), pos_step, jnp.zeros((1, TQ), jnp.int32))
        for c in range(S // KEY_ROWS):
            selected_bias(c, lambda key, kpos: (key > thr) | ((key == thr) & (kpos < bound)))

    aqT = aq_ref[...].astype(F32).T
    gq = gq_ref[...] * (ATT_DH ** -0.5 * LOG2_E)
    for h in range(ATT_HEADS):
        x = aqT[h * ATT_DH:(h + 1) * ATT_DH, :]
        r = lax.rsqrt(_reduce_rows(x * x, jnp.add, jnp.sum) * (1.0 / ATT_DH) + EPS)
        qt_ref[h * ATT_DH:(h + 1) * ATT_DH, :] = (x * r * gq).astype(BF16)

    n_pairs = ATT_HEADS // 2
    blocks = [slice(c * KEY_ROWS, (c + 1) * KEY_ROWS) for c in range(S // KEY_ROWS)]

    def logits_stage(hp, rows, q_pair, top):
        s = jnp.dot(kn_ref[rows, :], q_pair, preferred_element_type=F32) + bias_ref[rows, :]
        s_ref[hp % 2, rows, :] = s
        return jnp.maximum(top, _fold_rows(s, jnp.maximum))

    def probs_stage(hp, rows, top, den):
        p = jnp.exp2(s_ref[hp % 2, rows, :] - top)
        p_ref[hp % 2, rows, :] = p.astype(BF16)
        return den + _fold_rows(p, jnp.add)

    def values_stage(hp, rows, acc):
        return acc + jnp.dot(vt_ref[:, rows], p_ref[hp % 2, rows, :], preferred_element_type=F32)

    def query_pair(hp):
        row0 = hp * 2 * ATT_DH
        return jnp.concatenate([qt_ref[row0:row0 + ATT_DH, :], qt_ref[row0 + ATT_DH:row0 + 2 * ATT_DH, :]], axis=1)

    def finish(hp, acc, den):
        oT = acc / den
        row0 = hp * 2 * ATT_DH
        ot_ref[row0:row0 + ATT_DH, :] = oT[:, :TQ]
        ot_ref[row0 + ATT_DH:row0 + 2 * ATT_DH, :] = oT[:, TQ:]

    neg = jnp.full((8, 2 * TQ), -jnp.inf, F32)
    q_next = query_pair(0)
    top_next = neg
    for rows in blocks:
        top_next = logits_stage(0, rows, q_next, top_next)
    den_prev = None
    for hp in range(n_pairs):
        top = jnp.max(top_next, axis=0, keepdims=True)
        if hp + 1 < n_pairs:
            q_next = query_pair(hp + 1)
            top_next = neg
        den = jnp.zeros((8, 2 * TQ), F32)
        acc = jnp.zeros((ATT_DH, 2 * TQ), F32)
        for rows in blocks:
            if hp + 1 < n_pairs:
                top_next = logits_stage(hp + 1, rows, q_next, top_next)
            den = probs_stage(hp, rows, top, den)
            if hp > 0:
                acc = values_stage(hp - 1, rows, acc)
        if hp > 0:
            finish(hp - 1, acc, den_prev)
        den_prev = jnp.sum(den, axis=0, keepdims=True)
    acc = jnp.zeros((ATT_DH, 2 * TQ), F32)
    for rows in blocks:
        acc = values_stage(n_pairs - 1, rows, acc)
    finish(n_pairs - 1, acc, den_prev)

    o_ref[...] = (ot_ref[...].T * _silu(ag_ref[...].astype(F32))).astype(o_ref.dtype)


def _sparse_attention(p_main, p_small, keys, gq, layer, B, S):
    T = p_main.shape[0]
    W = ATT_HEADS * ATT_DH
    TQ = Q_BLOCK
    nb = S // TQ
    topk = min(INDEX_TOPK, S // 4)
    n_iq = IDX_HEADS * IDX_DH
    seg = n_iq // LANES
    kn, vt, ikn = keys
    group_keys = max(KEY_ROWS, S // 8)
    group_blocks = group_keys // TQ
    out = None
    for first_block in range(0, nb, group_blocks):
        n_keys = (first_block + group_blocks) * TQ
        qrow = lambda b, i, first_block=first_block: b * nb + first_block + i
        in_specs = [pl.BlockSpec((TQ, n_iq), lambda b, i: (qrow(b, i), 0)),
                    pl.BlockSpec((TQ, LANES), lambda b, i: (qrow(b, i), seg + 2)),
                    pl.BlockSpec((TQ, W), lambda b, i: (qrow(b, i), 4)),
                    pl.BlockSpec((TQ, W), lambda b, i: (qrow(b, i), 5)),
                    pl.BlockSpec((None, n_keys, ATT_DH), lambda b, i: (b, 0, 0)),
                    pl.BlockSpec((None, ATT_DH, n_keys), lambda b, i: (b, 0, 0)),
                    pl.BlockSpec((None, 2, n_keys, LANES), lambda b, i: (b, 0, 0, 0)),
                    pl.BlockSpec((None, ATT_DH, 1), lambda b, i: (layer, 0, 0))]
        args = [p_small, p_small, p_main, p_main, kn, vt, ikn, gq]
        aliases = {}
        if out is not None:
            in_specs.append(pl.BlockSpec(memory_space=pl.ANY))
            aliases = {len(args): 0}
            args.append(out)
        out = pl.pallas_call(
            functools.partial(_sparse_attention_kernel, topk=topk, first_block=first_block),
            grid=(B, group_blocks),
            in_specs=in_specs,
            out_specs=pl.BlockSpec((TQ, W), lambda b, i: (qrow(b, i), 0)),
            out_shape=jax.ShapeDtypeStruct((T, W), BF16),
            scratch_shapes=[pltpu.VMEM((n_keys, TQ), jnp.int32),
                            pltpu.VMEM((n_keys, 2 * TQ), F32),
                            pltpu.VMEM((W, TQ), BF16),
                            pltpu.VMEM((2, n_keys, 2 * TQ), F32),
                            pltpu.VMEM((2, n_keys, 2 * TQ), BF16),
                            pltpu.VMEM((W, TQ), F32)],
            input_output_aliases=aliases,
            compiler_params=_params(2),
            name="sparse_attention",
        )(*args)
    return out


def _merge_kernel(ar_ref, aa_ref, wr_ref, wa_ref, ga_ref, gb_ref, m_ref, wrb_ref, wab_ref):
    @pl.when(pl.program_id(1) == 0)
    def _():
        wrb_ref[...] = wr_ref[...].astype(BF16)
        wab_ref[...] = wa_ref[...].astype(BF16)

    u_ret = jnp.dot(ar_ref[...], wrb_ref[...], preferred_element_type=F32)
    u_att = jnp.dot(aa_ref[...], wab_ref[...], preferred_element_type=F32)
    m = jax.nn.sigmoid(ga_ref[...].astype(F32)) * u_ret + jax.nn.sigmoid(gb_ref[...].astype(F32)) * u_att
    m_ref[...] = m.astype(m_ref.dtype)


def _merge(a_ret, a_att, w_ret, w_att, p_gates, layer):
    T, K = a_ret.shape
    D = w_ret.shape[-1]
    tm = _tile(T, 512)
    tn = _tile(D, 512)
    return pl.pallas_call(
        _merge_kernel,
        grid=(D // tn, T // tm),
        in_specs=[pl.BlockSpec((tm, K), lambda j, i: (i, 0)),
                  pl.BlockSpec((tm, K), lambda j, i: (i, 0)),
                  pl.BlockSpec((None, K, tn), lambda j, i: (layer, 0, j)),
                  pl.BlockSpec((None, K, tn), lambda j, i: (layer, 0, j)),
                  pl.BlockSpec((tm, tn), lambda j, i: (i, j)),
                  pl.BlockSpec((tm, tn), lambda j, i: (i, D // tn + j))],
        out_specs=pl.BlockSpec((tm, tn), lambda j, i: (i, j)),
        out_shape=jax.ShapeDtypeStruct((T, D), BF16),
        scratch_shapes=[pltpu.VMEM((K, tn), BF16), pltpu.VMEM((K, tn), BF16)],
        compiler_params=_params(2),
        name="merge",
    )(a_ret, a_att, w_ret, w_att, p_gates, p_gates)


def _output_kernel(m_ref, w_ref, x_ref, *refs):
    wb_ref = refs[-1]

    @pl.when(pl.program_id(0) == 0)
    def _():
        wb_ref[...] = w_ref[...].astype(BF16)

    x = x_ref[...] + jnp.dot(m_ref[...], wb_ref[...], preferred_element_type=F32)
    if len(refs) == 2:
        refs[0][...] = x
    else:
        g_ref, xo_ref, h_ref = refs[:3]
        xo_ref[...] = x
        r = lax.rsqrt(jnp.mean(x * x, axis=-1, keepdims=True) + EPS)
        h_ref[...] = (x * r * g_ref[...]).astype(h_ref.dtype)


def _output(m, w_out, x, norm_g, layer, last):
    T, D = x.shape
    tm = _tile(T, 256)
    row_spec = pl.BlockSpec((tm, D), lambda i: (i, 0))
    w_spec = pl.BlockSpec((None, D, D), lambda i: (layer, 0, 0), pipeline_mode=pl.Buffered(1))
    in_specs = [row_spec, w_spec, row_spec]
    scratch = [pltpu.VMEM((D, D), BF16)]
    if last:
        return pl.pallas_call(
            _output_kernel, grid=(T // tm,), in_specs=in_specs, out_specs=row_spec,
            out_shape=jax.ShapeDtypeStruct((T, D), F32), scratch_shapes=scratch,
            compiler_params=_params(1), name="output_last",
        )(m, w_out, x), None
    return pl.pallas_call(
        _output_kernel, grid=(T // tm,),
        in_specs=in_specs + [pl.BlockSpec((None, 1, D), lambda i: (layer + 1, 0, 0))],
        out_specs=[row_spec, row_spec],
        out_shape=[jax.ShapeDtypeStruct((T, D), F32), jax.ShapeDtypeStruct((T, D), BF16)],
        scratch_shapes=scratch, compiler_params=_params(1), name="output",
    )(m, w_out, x, norm_g)


def kernel(x, norm_g, w_in, ret_out_g, att_q_g, att_k_g, idx_k_g, w_branch_ret, w_branch_att, w_out):
    B, S, D = x.shape
    depth = w_in.shape[0]
    T = B * S
    assert w_in.shape[-1] == N_IN and S % KEY_ROWS == 0
    norm_g3 = norm_g.reshape(depth, 1, D)
    ret_g4 = ret_out_g.reshape(depth, RET_HEADS, 1, RET_DV)
    gq = att_q_g.reshape(depth, ATT_DH, 1)
    gk = att_k_g.reshape(depth, 1, ATT_DH)
    gik = jnp.pad(idx_k_g, ((0, 0), (0, LANES - IDX_DH))).reshape(depth, 1, LANES)
    tables = _retention_tables(S, _tile(S, 256))

    xf = x.reshape(T, D)
    h = _rmsnorm(xf, norm_g3, 0)
    for layer in range(depth):
        p_main = _project_main(h, w_in, layer)
        p_gates = _project_gates(h, w_in, layer)
        p_small = _project_small(h, w_in, layer)
        a_ret = _retention(p_main, tables, ret_g4, layer, B, S)
        keys = _prepare_keys(p_small, gk, gik, layer, B, S)
        a_att = _sparse_attention(p_main, p_small, keys, gq, layer, B, S)
        m = _merge(a_ret, a_att, w_branch_ret, w_branch_att, p_gates, layer)
        xf, h = _output(m, w_out, xf, norm_g3, layer, last=layer == depth - 1)
    return xf.reshape(B, S, D)
```

```python
import functools

import jax
import jax.numpy as jnp
from jax import lax
from jax.experimental import pallas as pl
from jax.experimental.pallas import tpu as pltpu

F32 = jnp.float32
BF16 = jnp.bfloat16

CHUNK = 64
EPS = 1e-6
RET_HEADS = 8
RET_DK = 256
RET_DV = 256
ROT_BASE = 10000.0
ATT_HEADS = 16
ATT_DH = 128
IDX_HEADS = 16
IDX_DH = 64
INDEX_TOPK = 256
Q_BLOCK = 128
LANES = 128
KEY_ROWS = 256

INT_MIN = -(2 ** 31)
KEY_NEG_INF = -2139095041
KEY_POS_INF = 2139095040
MASKED_LOGIT = -1e30
LOG2_E = 1.4426950408889634

VMEM_LIMIT_BYTES = 56 * 1024 * 1024


def _params(n_axes):
    return pltpu.CompilerParams(dimension_semantics=("arbitrary",) * n_axes,
                                vmem_limit_bytes=VMEM_LIMIT_BYTES)


def _tile(n, pref):
    t = min(n, pref)
    assert n % t == 0, (n, t)
    return t


def _sigmoid(g):
    return 0.5 * jnp.tanh(0.5 * g) + 0.5


def _silu(g):
    half = 0.5 * g
    return half * jnp.tanh(half) + half


def _fold_rows(x, op):
    SUBLANES = 8
    while x.shape[0] > SUBLANES and x.shape[0] % (2 * SUBLANES) == 0:
        half = x.shape[0] // 2
        x = op(x[:half], x[half:])
    return x


def _reduce_rows(x, op, finish):
    return finish(_fold_rows(x, op), axis=0, keepdims=True)


def _rmsnorm_kernel(x_ref, g_ref, h_ref):
    x = x_ref[...]
    r = lax.rsqrt(jnp.mean(x * x, axis=-1, keepdims=True) + EPS)
    h_ref[...] = (x * r * g_ref[...]).astype(h_ref.dtype)


def _rmsnorm(x, norm_g, layer):
    T, D = x.shape
    tm = _tile(T, 512)
    return pl.pallas_call(
        _rmsnorm_kernel,
        grid=(T // tm,),
        in_specs=[pl.BlockSpec((tm, D), lambda i: (i, 0)),
                  pl.BlockSpec((None, 1, D), lambda i: (layer, 0, 0))],
        out_specs=pl.BlockSpec((tm, D), lambda i: (i, 0)),
        out_shape=jax.ShapeDtypeStruct((T, D), BF16),
        compiler_params=_params(1),
        name="rmsnorm",
    )(x, norm_g)


OFF_AQ = 4 * RET_HEADS * RET_DK
OFF_AK = OFF_AQ + ATT_HEADS * ATT_DH
OFF_AV = OFF_AK + ATT_DH
OFF_AG = OFF_AV + ATT_DH
OFF_IQ = OFF_AG + ATT_HEADS * ATT_DH
OFF_IK = OFF_IQ + IDX_HEADS * IDX_DH
OFF_IW = OFF_IK + IDX_DH
OFF_GA = OFF_IW + IDX_HEADS
N_IN = OFF_GA + 2 * RET_HEADS * RET_DK


def _project_kernel(a_ref, *refs):
    *w_refs, o_ref, wb_ref = refs
    K = wb_ref.shape[0]
    STRIP = 512

    @pl.when(pl.program_id(1) == 0)
    def _():
        for k0 in range(0, K, STRIP):
            parts = [r[0, :, k0:k0 + STRIP] for r in w_refs]
            w = parts[0] if len(parts) == 1 else jnp.concatenate(parts, axis=0)
            wb_ref[k0:k0 + STRIP, :] = w.T.astype(BF16)

    o_ref[...] = jnp.dot(a_ref[...], wb_ref[...], preferred_element_type=F32).astype(o_ref.dtype)


def _project(h, w_t, windows, n_tiles, name):
    T, K = h.shape
    tm = _tile(T, 1024)
    tn = sum(rows for rows, _ in windows)
    specs = [pl.BlockSpec((pl.Element(1), pl.Element(rows), pl.Element(K)), index_map)
             for rows, index_map in windows]
    return pl.pallas_call(
        _project_kernel,
        grid=(n_tiles, T // tm),
        in_specs=[pl.BlockSpec((tm, K), lambda j, i: (i, 0))] + specs,
        out_specs=pl.BlockSpec((tm, tn), lambda j, i: (i, j)),
        out_shape=jax.ShapeDtypeStruct((T, n_tiles * tn), BF16),
        scratch_shapes=[pltpu.VMEM((K, tn), BF16)],
        compiler_params=_params(2),
        name=name,
    )(h, *([w_t] * len(specs)))


def _project_main(h, w_t, layer):
    tn = 1024
    SUB = 8
    runs = ((0, OFF_AK), (OFF_AG, OFF_IQ), (OFF_GA, N_IN))
    assert all(a % SUB == 0 and (b - a) % tn == 0 for a, b in runs)
    tiles_before = [sum((b - a) // tn for a, b in runs[:r]) for r in range(len(runs) + 1)]

    def first_row(j):
        row = j * (tn // SUB)
        for r in range(1, len(runs)):
            gap = runs[r][0] - runs[r - 1][1]
            row = row + jnp.where(j >= tiles_before[r], gap // SUB, 0)
        return row * SUB

    return _project(h, w_t, [(tn, lambda j, i: (layer, first_row(j), 0))], tiles_before[-1], "project_main")


def _project_small(h, w_t, layer):
    windows = [(IDX_HEADS * IDX_DH, lambda j, i: (layer, OFF_IQ, 0)),
               (2 * ATT_DH, lambda j, i: (layer, OFF_AK, 0)),
               (LANES, lambda j, i: (layer, OFF_IK, 0))]
    return _project(h, w_t, windows, 1, "project_small")


def _retention_tables(S, R):
    pos = jnp.arange(S, dtype=F32)
    inv = 1.0 / (ROT_BASE ** jnp.linspace(0.0, 1.0, RET_DK // 2, dtype=F32))
    ang = pos[:, None] * inv[None, :]
    log_g = jnp.log(1.0 - 2.0 ** (-5.0 - jnp.arange(RET_HEADS, dtype=F32)))[:, None, None]
    idx = jnp.arange(R, dtype=F32)
    chunk = jnp.arange(R) // CHUNK
    visible = chunk[None, :] <= chunk[:, None]
    dm = jnp.where(visible[None], jnp.exp(jnp.abs(idx[:, None] - idx[None, :])[None] * log_g), 0.0)
    ones = jnp.ones((1, 1, RET_DV), F32)
    qd = jnp.exp((idx[None, :, None] + 1.0) * log_g) * ones
    kd = jnp.exp((R - 1.0 - idx)[None, :, None] * log_g) * ones
    cd = jnp.exp(float(R) * log_g) * ones
    return jnp.cos(ang), jnp.sin(ang), dm, qd, kd, cd


def _retention_kernel(q_ref, k_ref, v_ref, g_ref, cos_ref, sin_ref, dm_ref, qd_ref, kd_ref, cd_ref,
                      og_ref, o_ref, state_ref):
    @pl.when(pl.program_id(1) == 0)
    def _():
        state_ref[...] = jnp.zeros_like(state_ref)

    cos = cos_ref[...]
    sin = sin_ref[...]
    half = RET_DK // 2

    def rotated(ref, h):
        x1 = ref[:, h * RET_DK:h * RET_DK + half].astype(F32)
        x2 = ref[:, h * RET_DK + half:(h + 1) * RET_DK].astype(F32)
        return jnp.concatenate([x1 * cos - x2 * sin, x2 * cos + x1 * sin], axis=1)

    for h in range(RET_HEADS):
        cols = slice(h * RET_DV, (h + 1) * RET_DV)
        qr = rotated(q_ref, h)
        kr = rotated(k_ref, h) * (RET_DK ** -0.5)
        qb = qr.astype(BF16)
        kb = kr.astype(BF16)
        v = v_ref[:, cols]
        s = lax.dot_general(qb, kb, (((1,), (1,)), ((), ())), preferred_element_type=F32) * dm_ref[h]
        o = jnp.dot(s.astype(BF16), v, preferred_element_type=F32)
        state = state_ref[h]
        o = o + jnp.dot(qb, state.astype(BF16), preferred_element_type=F32) * qd_ref[h]
        kdec = (kr * kd_ref[h]).astype(BF16)
        state_ref[h] = state * cd_ref[h] + lax.dot_general(
            kdec, v, (((0,), (0,)), ((), ())), preferred_element_type=F32)
        r = lax.rsqrt(jnp.mean(o * o, axis=-1, keepdims=True) + EPS)
        g = g_ref[:, cols].astype(F32)
        o_ref[:, cols] = (o * r * og_ref[h] * _silu(g)).astype(o_ref.dtype)


def _retention(p_main, tables, ret_out_g, layer, B, S):
    T = p_main.shape[0]
    W = RET_HEADS * RET_DV
    R = tables[2].shape[1]
    nr = S // R
    cos, sin, dm, qd, kd, cd = tables
    row = lambda b, i: b * nr + i
    full3 = lambda a: pl.BlockSpec(a.shape, lambda b, i: (0, 0, 0))
    return pl.pallas_call(
        _retention_kernel,
        grid=(B, nr),
        in_specs=[pl.BlockSpec((R, W), lambda b, i: (row(b, i), 0)),
                  pl.BlockSpec((R, W), lambda b, i: (row(b, i), 1)),
                  pl.BlockSpec((R, W), lambda b, i: (row(b, i), 2)),
                  pl.BlockSpec((R, W), lambda b, i: (row(b, i), 3)),
                  pl.BlockSpec((R, RET_DK // 2), lambda b, i: (i, 0)),
                  pl.BlockSpec((R, RET_DK // 2), lambda b, i: (i, 0)),
                  full3(dm), full3(qd), full3(kd), full3(cd),
                  pl.BlockSpec((None, RET_HEADS, 1, RET_DV), lambda b, i: (layer, 0, 0, 0))],
        out_specs=pl.BlockSpec((R, W), lambda b, i: (row(b, i), 0)),
        out_shape=jax.ShapeDtypeStruct((T, W), BF16),
        scratch_shapes=[pltpu.VMEM((RET_HEADS, RET_DK, RET_DV), F32)],
        compiler_params=_params(2),
        name="retention",
    )(p_main, p_main, p_main, p_main, cos, sin, dm, qd, kd, cd, ret_out_g)


def _prepare_keys_kernel(ak_ref, av_ref, ik_ref, gk_ref, gik_ref, kn_ref, vt_ref, ikn_ref):
    ak = ak_ref[...].astype(F32)
    r = lax.rsqrt(jnp.mean(ak * ak, axis=-1, keepdims=True) + EPS)
    kn_ref[...] = (ak * r * gk_ref[...]).astype(BF16)
    vt_ref[...] = av_ref[...].astype(F32).T.astype(BF16)
    ik = ik_ref[...].astype(F32)
    ik = jnp.where(lax.broadcasted_iota(jnp.int32, ik.shape, 1) < IDX_DH, ik, 0.0)
    r = lax.rsqrt(jnp.sum(ik * ik, axis=-1, keepdims=True) * (1.0 / IDX_DH) + EPS)
    ikn = ik * r * gik_ref[...]
    ikn_ref[0] = ikn.astype(BF16)
    ikn_ref[1] = pltpu.roll(ikn, IDX_DH, 1).astype(BF16)


def _prepare_keys(p_small, gk, gik, layer, B, S):
    seg = IDX_HEADS * IDX_DH // LANES
    return pl.pallas_call(
        _prepare_keys_kernel,
        grid=(B,),
        in_specs=[pl.BlockSpec((S, LANES), lambda b: (b, seg)),
                  pl.BlockSpec((S, LANES), lambda b: (b, seg + 1)),
                  pl.BlockSpec((S, LANES), lambda b: (b, seg + 2)),
                  pl.BlockSpec((None, 1, ATT_DH), lambda b: (layer, 0, 0)),
                  pl.BlockSpec((None, 1, LANES), lambda b: (layer, 0, 0))],
        out_specs=[pl.BlockSpec((None, S, ATT_DH), lambda b: (b, 0, 0)),
                   pl.BlockSpec((None, ATT_DH, S), lambda b: (b, 0, 0)),
                   pl.BlockSpec((None, 2, S, LANES), lambda b: (b, 0, 0, 0))],
        out_shape=[jax.ShapeDtypeStruct((B, S, ATT_DH), BF16),
                   jax.ShapeDtypeStruct((B, ATT_DH, S), BF16),
                   jax.ShapeDtypeStruct((B, 2, S, LANES), BF16)],
        compiler_params=_params(1),
        name="prepare_keys",
    )(p_small, p_small, p_small, gk, gik)


def _sparse_attention_kernel(iq_ref, iw_ref, aq_ref, ag_ref, kn_ref, vt_ref, ikn_ref, gq_ref, *rest,
                             topk, first_block):
    o_ref, key_ref, bias_ref, qt_ref, s_ref, p_ref, ot_ref = rest[-7:]
    S = kn_ref.shape[0]
    TQ = iq_ref.shape[0]
    i = pl.program_id(1) + first_block

    iqT = iq_ref[...].astype(F32).T.astype(BF16)
    wT = iw_ref[...].astype(F32).T[IDX_DH:IDX_DH + IDX_HEADS, :] * (IDX_HEADS ** -0.5 * IDX_DH ** -0.5)
    score = jnp.zeros((S, TQ), F32)
    for hp in range(IDX_HEADS // 2):
        slab = iqT[hp * LANES:(hp + 1) * LANES, :]
        s0 = jnp.dot(ikn_ref[0], slab, preferred_element_type=F32)
        s1 = jnp.dot(ikn_ref[1], slab, preferred_element_type=F32)
        score = score + jnp.maximum(s0, 0.0) * wT[2 * hp:2 * hp + 1, :]
        score = score + jnp.maximum(s1, 0.0) * wT[2 * hp + 1:2 * hp + 2, :]

    bits = lax.bitcast_convert_type(score, jnp.int32)
    key = bits ^ ((bits >> 31) & 0x7FFFFFFF)
    kpos = lax.broadcasted_iota(jnp.int32, (S, TQ), 0)
    qpos = lax.broadcasted_iota(jnp.int32, (S, TQ), 1) + i * TQ
    key = jnp.where(kpos // CHUNK <= qpos // CHUNK, key, INT_MIN)
    key_ref[...] = key

    def count(pred):
        acc = jnp.zeros((KEY_ROWS, TQ), F32)
        for c in range(S // KEY_ROWS):
            rows = slice(c * KEY_ROWS, (c + 1) * KEY_ROWS)
            kpos = lax.broadcasted_iota(jnp.int32, (KEY_ROWS, TQ), 0) + c * KEY_ROWS
            acc = acc + jnp.where(pred(key_ref[rows, :], kpos), 1.0, 0.0)
        return _reduce_rows(acc, jnp.add, jnp.sum)

    def try_threshold(cand, thr, cnt):
        c = count(lambda key, kpos: key >= cand)
        ok = c >= topk
        return jnp.where(ok, cand, thr), jnp.where(ok, c, cnt)

    thr = jnp.full((1, TQ), INT_MIN, jnp.int32)
    cnt = jnp.full((1, TQ), float(S), F32)
    thr, cnt = try_threshold(jnp.zeros((1, TQ), jnp.int32), thr, cnt)

    def bit_step(it, carry):
        thr, cnt = carry
        return try_threshold(thr | (1 << (30 - it)), thr, cnt)

    thr, cnt = lax.fori_loop(0, 31, bit_step, (thr, cnt))

    def selected_bias(c, selected):
        key = key_ref[c * KEY_ROWS:(c + 1) * KEY_ROWS, :]
        kpos = lax.broadcasted_iota(jnp.int32, (KEY_ROWS, TQ), 0) + c * KEY_ROWS
        finite = (key > KEY_NEG_INF) & (key < KEY_POS_INF)
        b = jnp.where(selected(key, kpos) & finite, 0.0, MASKED_LOGIT)
        bias_ref[c * KEY_ROWS:(c + 1) * KEY_ROWS, :] = jnp.concatenate([b, b], axis=1)

    for c in range(S // KEY_ROWS):
        selected_bias(c, lambda key, kpos: key >= thr)

    tied = jnp.where((cnt > topk) & (thr > KEY_NEG_INF), 1.0, 0.0)

    @pl.when(jnp.max(tied) > 0.0)
    def _break_ties():
        need = topk - count(lambda key, kpos: key > thr)

        def pos_step(it, bound):
            cand = bound | (1 << (S.bit_length() - 1 - it))
            ok = count(lambda key, kpos: (key == thr) & (kpos < cand)) <= need
            return jnp.where(ok, cand, bound)

        bound = lax.fori_loop(0, S.bit_length(), pos_step, jnp.zeros((1, TQ), jnp.int32))
        for c in range(S // KEY_ROWS):
            selected_bias(c, lambda key, kpos: (key > thr) | ((key == thr) & (kpos < bound)))

    aqT = aq_ref[...].astype(F32).T
    gq = gq_ref[...] * (ATT_DH ** -0.5 * LOG2_E)
    for h in range(ATT_HEADS):
        x = aqT[h * ATT_DH:(h + 1) * ATT_DH, :]
        r = lax.rsqrt(_reduce_rows(x * x, jnp.add, jnp.sum) * (1.0 / ATT_DH) + EPS)
        qt_ref[h * ATT_DH:(h + 1) * ATT_DH, :] = (x * r * gq).astype(BF16)

    n_pairs = ATT_HEADS // 2
    blocks = [slice(c * KEY_ROWS, (c + 1) * KEY_ROWS) for c in range(S // KEY_ROWS)]

    def logits_stage(hp, rows, q_pair, top):
        s = jnp.dot(kn_ref[rows, :], q_pair, preferred_element_type=F32) + bias_ref[rows, :]
        s_ref[hp % 2, rows, :] = s
        return jnp.maximum(top, _fold_rows(s, jnp.maximum))

    def probs_stage(hp, rows, top, den):
        p = jnp.exp2(s_ref[hp % 2, rows, :] - top)
        p_ref[hp % 2, rows, :] = p.astype(BF16)
        return den + _fold_rows(p, jnp.add)

    def values_stage(hp, rows, acc):
        return acc + jnp.dot(vt_ref[:, rows], p_ref[hp % 2, rows, :], preferred_element_type=F32)

    def query_pair(hp):
        row0 = hp * 2 * ATT_DH
        return jnp.concatenate([qt_ref[row0:row0 + ATT_DH, :], qt_ref[row0 + ATT_DH:row0 + 2 * ATT_DH, :]], axis=1)

    def finish(hp, acc, den):
        oT = acc / den
        row0 = hp * 2 * ATT_DH
        ot_ref[row0:row0 + ATT_DH, :] = oT[:, :TQ]
        ot_ref[row0 + ATT_DH:row0 + 2 * ATT_DH, :] = oT[:, TQ:]

    neg = jnp.full((8, 2 * TQ), -jnp.inf, F32)
    q_next = query_pair(0)
    top_next = neg
    for rows in blocks:
        top_next = logits_stage(0, rows, q_next, top_next)
    den_prev = None
    for hp in range(n_pairs):
        top = jnp.max(top_next, axis=0, keepdims=True)
        if hp + 1 < n_pairs:
            q_next = query_pair(hp + 1)
            top_next = neg
        den = jnp.zeros((8, 2 * TQ), F32)
        acc = jnp.zeros((ATT_DH, 2 * TQ), F32)
        for rows in blocks:
            if hp + 1 < n_pairs:
                top_next = logits_stage(hp + 1, rows, q_next, top_next)
            den = probs_stage(hp, rows, top, den)
            if hp > 0:
                acc = values_stage(hp - 1, rows, acc)
        if hp > 0:
            finish(hp - 1, acc, den_prev)
        den_prev = jnp.sum(den, axis=0, keepdims=True)
    acc = jnp.zeros((ATT_DH, 2 * TQ), F32)
    for rows in blocks:
        acc = values_stage(n_pairs - 1, rows, acc)
    finish(n_pairs - 1, acc, den_prev)

    o_ref[...] = (ot_ref[...].T * _silu(ag_ref[...].astype(F32))).astype(o_ref.dtype)


def _sparse_attention(p_main, p_small, keys, gq, layer, B, S):
    T = p_main.shape[0]
    W = ATT_HEADS * ATT_DH
    TQ = Q_BLOCK
    nb = S // TQ
    topk = min(INDEX_TOPK, S // 4)
    n_iq = IDX_HEADS * IDX_DH
    seg = n_iq // LANES
    kn, vt, ikn = keys
    group_keys = max(KEY_ROWS, S // 8)
    group_blocks = group_keys // TQ
    out = None
    for first_block in range(0, nb, group_blocks):
        n_keys = (first_block + group_blocks) * TQ
        qrow = lambda b, i, first_block=first_block: b * nb + first_block + i
        in_specs = [pl.BlockSpec((TQ, n_iq), lambda b, i: (qrow(b, i), 0)),
                    pl.BlockSpec((TQ, LANES), lambda b, i: (qrow(b, i), seg + 2)),
                    pl.BlockSpec((TQ, W), lambda b, i: (qrow(b, i), 4)),
                    pl.BlockSpec((TQ, W), lambda b, i: (qrow(b, i), 5)),
                    pl.BlockSpec((None, n_keys, ATT_DH), lambda b, i: (b, 0, 0)),
                    pl.BlockSpec((None, ATT_DH, n_keys), lambda b, i: (b, 0, 0)),
                    pl.BlockSpec((None, 2, n_keys, LANES), lambda b, i: (b, 0, 0, 0)),
                    pl.BlockSpec((None, ATT_DH, 1), lambda b, i: (layer, 0, 0))]
        args = [p_small, p_small, p_main, p_main, kn, vt, ikn, gq]
        aliases = {}
        if out is not None:
            in_specs.append(pl.BlockSpec(memory_space=pl.ANY))
            aliases = {len(args): 0}
            args.append(out)
        out = pl.pallas_call(
            functools.partial(_sparse_attention_kernel, topk=topk, first_block=first_block),
            grid=(B, group_blocks),
            in_specs=in_specs,
            out_specs=pl.BlockSpec((TQ, W), lambda b, i: (qrow(b, i), 0)),
            out_shape=jax.ShapeDtypeStruct((T, W), BF16),
            scratch_shapes=[pltpu.VMEM((n_keys, TQ), jnp.int32),
                            pltpu.VMEM((n_keys, 2 * TQ), F32),
                            pltpu.VMEM((W, TQ), BF16),
                            pltpu.VMEM((2, n_keys, 2 * TQ), F32),
                            pltpu.VMEM((2, n_keys, 2 * TQ), BF16),
                            pltpu.VMEM((W, TQ), F32)],
            input_output_aliases=aliases,
            compiler_params=_params(2),
            name="sparse_attention",
        )(*args)
    return out


def _merge_kernel(ar_ref, aa_ref, wr_ref, wa_ref, ga_ref, gb_ref, m_ref, wrb_ref, wab_ref):
    @pl.when(pl.program_id(1) == 0)
    def _():
        wrb_ref[...] = wr_ref[...].astype(BF16)
        wab_ref[...] = wa_ref[...].astype(BF16)

    u_ret = jnp.dot(ar_ref[...], wrb_ref[...], preferred_element_type=F32)
    u_att = jnp.dot(aa_ref[...], wab_ref[...], preferred_element_type=F32)
    m = _sigmoid(ga_ref[...].astype(F32)) * u_ret + _sigmoid(gb_ref[...].astype(F32)) * u_att
    m_ref[...] = m.astype(m_ref.dtype)


def _merge(a_ret, a_att, w_ret, w_att, p_main, layer):
    T, K = a_ret.shape
    D = w_ret.shape[-1]
    tm = _tile(T, 512)
    tn = _tile(D, 512)
    gate_a = 6 * (D // tn)
    gate_b = 7 * (D // tn)
    return pl.pallas_call(
        _merge_kernel,
        grid=(D // tn, T // tm),
        in_specs=[pl.BlockSpec((tm, K), lambda j, i: (i, 0)),
                  pl.BlockSpec((tm, K), lambda j, i: (i, 0)),
                  pl.BlockSpec((None, K, tn), lambda j, i: (layer, 0, j)),
                  pl.BlockSpec((None, K, tn), lambda j, i: (layer, 0, j)),
                  pl.BlockSpec((tm, tn), lambda j, i: (i, gate_a + j)),
                  pl.BlockSpec((tm, tn), lambda j, i: (i, gate_b + j))],
        out_specs=pl.BlockSpec((tm, tn), lambda j, i: (i, j)),
        out_shape=jax.ShapeDtypeStruct((T, D), BF16),
        scratch_shapes=[pltpu.VMEM((K, tn), BF16), pltpu.VMEM((K, tn), BF16)],
        compiler_params=_params(2),
        name="merge",
    )(a_ret, a_att, w_ret, w_att, p_main, p_main)


def _output_kernel(m_ref, w_ref, x_ref, *refs):
    wb_ref = refs[-1]

    @pl.when(pl.program_id(0) == 0)
    def _():
        wb_ref[...] = w_ref[...].astype(BF16)

    x = x_ref[...] + jnp.dot(m_ref[...], wb_ref[...], preferred_element_type=F32)
    if len(refs) == 2:
        refs[0][...] = x
    else:
        g_ref, xo_ref, h_ref = refs[:3]
        xo_ref[...] = x
        r = lax.rsqrt(jnp.mean(x * x, axis=-1, keepdims=True) + EPS)
        h_ref[...] = (x * r * g_ref[...]).astype(h_ref.dtype)


def _output(m, w_out, x, norm_g, layer, last):
    T, D = x.shape
    tm = _tile(T, 256)
    row_spec = pl.BlockSpec((tm, D), lambda i: (i, 0))
    w_spec = pl.BlockSpec((None, D, D), lambda i: (layer, 0, 0), pipeline_mode=pl.Buffered(1))
    in_specs = [row_spec, w_spec, row_spec]
    scratch = [pltpu.VMEM((D, D), BF16)]
    if last:
        return pl.pallas_call(
            _output_kernel, grid=(T // tm,), in_specs=in_specs, out_specs=row_spec,
            out_shape=jax.ShapeDtypeStruct((T, D), F32), scratch_shapes=scratch,
            compiler_params=_params(1), name="output_last",
        )(m, w_out, x), None
    return pl.pallas_call(
        _output_kernel, grid=(T // tm,),
        in_specs=in_specs + [pl.BlockSpec((None, 1, D), lambda i: (layer + 1, 0, 0))],
        out_specs=[row_spec, row_spec],
        out_shape=[jax.ShapeDtypeStruct((T, D), F32), jax.ShapeDtypeStruct((T, D), BF16)],
        scratch_shapes=scratch, compiler_params=_params(1), name="output",
    )(m, w_out, x, norm_g)


def kernel(x, norm_g, w_in, ret_out_g, att_q_g, att_k_g, idx_k_g, w_branch_ret, w_branch_att, w_out):
    B, S, D = x.shape
    depth = w_in.shape[0]
    T = B * S
    assert w_in.shape[-1] == N_IN and S % KEY_ROWS == 0
    norm_g3 = norm_g.reshape(depth, 1, D)
    ret_g4 = ret_out_g.reshape(depth, RET_HEADS, 1, RET_DV)
    gq = att_q_g.reshape(depth, ATT_DH, 1)
    gk = att_k_g.reshape(depth, 1, ATT_DH)
    gik = jnp.pad(idx_k_g, ((0, 0), (0, LANES - IDX_DH))).reshape(depth, 1, LANES)
    tables = _retention_tables(S, _tile(S, 256))
    w_t = jnp.swapaxes(w_in, 1, 2)

    xf = x.reshape(T, D)
    h = _rmsnorm(xf, norm_g3, 0)
    for layer in range(depth):
        p_main = _project_main(h, w_t, layer)
        p_small = _project_small(h, w_t, layer)
        a_ret = _retention(p_main, tables, ret_g4, layer, B, S)
        keys = _prepare_keys(p_small, gk, gik, layer, B, S)
        a_att = _sparse_attention(p_main, p_small, keys, gq, layer, B, S)
        m = _merge(a_ret, a_att, w_branch_ret, w_branch_att, p_main, layer)
        xf, h = _output(m, w_out, xf, norm_g3, layer, last=layer == depth - 1)
    return xf.reshape(B, S, D)
```

```python
import functools

import jax
import jax.numpy as jnp
from jax import lax
from jax.experimental import pallas as pl
from jax.experimental.pallas import tpu as pltpu

F32 = jnp.float32
BF16 = jnp.bfloat16

CHUNK = 64
EPS = 1e-6
RET_HEADS = 8
RET_DK = 256
RET_DV = 256
ROT_BASE = 10000.0
ATT_HEADS = 16
ATT_DH = 128
IDX_HEADS = 16
IDX_DH = 64
INDEX_TOPK = 256
Q_BLOCK = 128
LANES = 128
KEY_ROWS = 256

INT_MIN = -(2 ** 31)
KEY_NEG_INF = -2139095041
KEY_POS_INF = 2139095040
MASKED_LOGIT = -1e30
LOG2_E = 1.4426950408889634

VMEM_LIMIT_BYTES = 56 * 1024 * 1024


def _params(n_axes):
    return pltpu.CompilerParams(dimension_semantics=("arbitrary",) * n_axes,
                                vmem_limit_bytes=VMEM_LIMIT_BYTES)


def _tile(n, pref):
    t = min(n, pref)
    assert n % t == 0, (n, t)
    return t


def _sigmoid(g):
    return 0.5 * jnp.tanh(0.5 * g) + 0.5


def _silu(g):
    half = 0.5 * g
    return half * jnp.tanh(half) + half


def _fold_rows(x, op):
    SUBLANES = 8
    while x.shape[0] > SUBLANES and x.shape[0] % (2 * SUBLANES) == 0:
        half = x.shape[0] // 2
        x = op(x[:half], x[half:])
    return x


def _reduce_rows(x, op, finish):
    return finish(_fold_rows(x, op), axis=0, keepdims=True)


def _rmsnorm_kernel(x_ref, g_ref, h_ref):
    x = x_ref[...]
    r = lax.rsqrt(jnp.mean(x * x, axis=-1, keepdims=True) + EPS)
    h_ref[...] = (x * r * g_ref[...]).astype(h_ref.dtype)


def _rmsnorm(x, norm_g, layer):
    T, D = x.shape
    tm = _tile(T, 512)
    return pl.pallas_call(
        _rmsnorm_kernel,
        grid=(T // tm,),
        in_specs=[pl.BlockSpec((tm, D), lambda i: (i, 0)),
                  pl.BlockSpec((None, 1, D), lambda i: (layer, 0, 0))],
        out_specs=pl.BlockSpec((tm, D), lambda i: (i, 0)),
        out_shape=jax.ShapeDtypeStruct((T, D), BF16),
        compiler_params=_params(1),
        name="rmsnorm",
    )(x, norm_g)


OFF_AQ = 4 * RET_HEADS * RET_DK
OFF_AK = OFF_AQ + ATT_HEADS * ATT_DH
OFF_AV = OFF_AK + ATT_DH
OFF_AG = OFF_AV + ATT_DH
OFF_IQ = OFF_AG + ATT_HEADS * ATT_DH
OFF_IK = OFF_IQ + IDX_HEADS * IDX_DH
OFF_IW = OFF_IK + IDX_DH
OFF_GA = OFF_IW + IDX_HEADS
N_IN = OFF_GA + 2 * RET_HEADS * RET_DK


def _project_kernel(a_ref, *refs):
    *w_refs, o_ref, wb_ref = refs
    K = wb_ref.shape[0]
    STRIP = 512

    @pl.when(pl.program_id(1) == 0)
    def _():
        for k0 in range(0, K, STRIP):
            parts = [r[0, :, k0:k0 + STRIP] for r in w_refs]
            w = parts[0] if len(parts) == 1 else jnp.concatenate(parts, axis=0)
            wb_ref[k0:k0 + STRIP, :] = w.T.astype(BF16)

    o_ref[...] = jnp.dot(a_ref[...], wb_ref[...], preferred_element_type=F32).astype(o_ref.dtype)


def _project(h, w_t, windows, n_tiles, name):
    T, K = h.shape
    tm = _tile(T, 1024)
    tn = sum(rows for rows, _ in windows)
    specs = [pl.BlockSpec((pl.Element(1), pl.Element(rows), pl.Element(K)), index_map)
             for rows, index_map in windows]
    return pl.pallas_call(
        _project_kernel,
        grid=(n_tiles, T // tm),
        in_specs=[pl.BlockSpec((tm, K), lambda j, i: (i, 0))] + specs,
        out_specs=pl.BlockSpec((tm, tn), lambda j, i: (i, j)),
        out_shape=jax.ShapeDtypeStruct((T, n_tiles * tn), BF16),
        scratch_shapes=[pltpu.VMEM((K, tn), BF16)],
        compiler_params=_params(2),
        name=name,
    )(h, *([w_t] * len(specs)))


def _project_main(h, w_t, layer):
    tn = 1024
    SUB = 8
    runs = ((0, OFF_AK), (OFF_AG, OFF_IQ), (OFF_GA, N_IN))
    assert all(a % SUB == 0 and (b - a) % tn == 0 for a, b in runs)
    tiles_before = [sum((b - a) // tn for a, b in runs[:r]) for r in range(len(runs) + 1)]

    def first_row(j):
        row = j * (tn // SUB)
        for r in range(1, len(runs)):
            gap = runs[r][0] - runs[r - 1][1]
            row = row + jnp.where(j >= tiles_before[r], gap // SUB, 0)
        return row * SUB

    return _project(h, w_t, [(tn, lambda j, i: (layer, first_row(j), 0))], tiles_before[-1], "project_main")


def _project_small(h, w_t, layer):
    windows = [(IDX_HEADS * IDX_DH, lambda j, i: (layer, OFF_IQ, 0)),
               (2 * ATT_DH, lambda j, i: (layer, OFF_AK, 0)),
               (LANES, lambda j, i: (layer, OFF_IK, 0))]
    return _project(h, w_t, windows, 1, "project_small")


def _retention_tables(S, R):
    pos = jnp.arange(S, dtype=F32)
    inv = 1.0 / (ROT_BASE ** jnp.linspace(0.0, 1.0, RET_DK // 2, dtype=F32))
    ang = pos[:, None] * inv[None, :]
    log_g = jnp.log(1.0 - 2.0 ** (-5.0 - jnp.arange(RET_HEADS, dtype=F32)))[:, None, None]
    idx = jnp.arange(R, dtype=F32)
    chunk = jnp.arange(R) // CHUNK
    visible = chunk[None, :] <= chunk[:, None]
    dm = jnp.where(visible[None], jnp.exp(jnp.abs(idx[:, None] - idx[None, :])[None] * log_g), 0.0)
    ones = jnp.ones((1, 1, RET_DV), F32)
    qd = jnp.exp((idx[None, :, None] + 1.0) * log_g) * ones
    kd = jnp.exp((R - 1.0 - idx)[None, :, None] * log_g) * ones
    cd = jnp.exp(float(R) * log_g) * ones
    return jnp.cos(ang), jnp.sin(ang), dm, qd, kd, cd


def _retention_kernel(q_ref, k_ref, v_ref, g_ref, cos_ref, sin_ref, dm_ref, qd_ref, kd_ref, cd_ref,
                      og_ref, o_ref, state_ref):
    @pl.when(pl.program_id(1) == 0)
    def _():
        state_ref[...] = jnp.zeros_like(state_ref)

    cos = cos_ref[...]
    sin = sin_ref[...]
    half = RET_DK // 2

    def rotated(ref, h):
        x1 = ref[:, h * RET_DK:h * RET_DK + half].astype(F32)
        x2 = ref[:, h * RET_DK + half:(h + 1) * RET_DK].astype(F32)
        return jnp.concatenate([x1 * cos - x2 * sin, x2 * cos + x1 * sin], axis=1)

    for h in range(RET_HEADS):
        cols = slice(h * RET_DV, (h + 1) * RET_DV)
        qr = rotated(q_ref, h)
        kr = rotated(k_ref, h) * (RET_DK ** -0.5)
        qb = qr.astype(BF16)
        kb = kr.astype(BF16)
        v = v_ref[:, cols]
        s = lax.dot_general(qb, kb, (((1,), (1,)), ((), ())), preferred_element_type=F32) * dm_ref[h]
        o = jnp.dot(s.astype(BF16), v, preferred_element_type=F32)
        state = state_ref[h]
        o = o + jnp.dot(qb, state.astype(BF16), preferred_element_type=F32) * qd_ref[h]
        kdec = (kr * kd_ref[h]).astype(BF16)
        state_ref[h] = state * cd_ref[h] + lax.dot_general(
            kdec, v, (((0,), (0,)), ((), ())), preferred_element_type=F32)
        r = lax.rsqrt(jnp.mean(o * o, axis=-1, keepdims=True) + EPS)
        g = g_ref[:, cols].astype(F32)
        o_ref[:, cols] = (o * r * og_ref[h] * _silu(g)).astype(o_ref.dtype)


def _retention(p_main, tables, ret_out_g, layer, B, S):
    T = p_main.shape[0]
    W = RET_HEADS * RET_DV
    R = tables[2].shape[1]
    nr = S // R
    cos, sin, dm, qd, kd, cd = tables
    row = lambda b, i: b * nr + i
    full3 = lambda a: pl.BlockSpec(a.shape, lambda b, i: (0, 0, 0))
    return pl.pallas_call(
        _retention_kernel,
        grid=(B, nr),
        in_specs=[pl.BlockSpec((R, W), lambda b, i: (row(b, i), 0)),
                  pl.BlockSpec((R, W), lambda b, i: (row(b, i), 1)),
                  pl.BlockSpec((R, W), lambda b, i: (row(b, i), 2)),
                  pl.BlockSpec((R, W), lambda b, i: (row(b, i), 3)),
                  pl.BlockSpec((R, RET_DK // 2), lambda b, i: (i, 0)),
                  pl.BlockSpec((R, RET_DK // 2), lambda b, i: (i, 0)),
                  full3(dm), full3(qd), full3(kd), full3(cd),
                  pl.BlockSpec((None, RET_HEADS, 1, RET_DV), lambda b, i: (layer, 0, 0, 0))],
        out_specs=pl.BlockSpec((R, W), lambda b, i: (row(b, i), 0)),
        out_shape=jax.ShapeDtypeStruct((T, W), BF16),
        scratch_shapes=[pltpu.VMEM((RET_HEADS, RET_DK, RET_DV), F32)],
        compiler_params=_params(2),
        name="retention",
    )(p_main, p_main, p_main, p_main, cos, sin, dm, qd, kd, cd, ret_out_g)


def _prepare_keys_kernel(ak_ref, av_ref, ik_ref, gk_ref, gik_ref, kn_ref, vt_ref, ikn_ref):
    ak = ak_ref[...].astype(F32)
    r = lax.rsqrt(jnp.mean(ak * ak, axis=-1, keepdims=True) + EPS)
    kn_ref[...] = (ak * r * gk_ref[...]).astype(BF16)
    vt_ref[...] = av_ref[...].astype(F32).T.astype(BF16)
    ik = ik_ref[...].astype(F32)
    ik = jnp.where(lax.broadcasted_iota(jnp.int32, ik.shape, 1) < IDX_DH, ik, 0.0)
    r = lax.rsqrt(jnp.sum(ik * ik, axis=-1, keepdims=True) * (1.0 / IDX_DH) + EPS)
    ikn_ref[...] = (ik * r * gik_ref[...]).astype(BF16)


def _prepare_keys(p_small, gk, gik, layer, B, S):
    seg = IDX_HEADS * IDX_DH // LANES
    return pl.pallas_call(
        _prepare_keys_kernel,
        grid=(B,),
        in_specs=[pl.BlockSpec((S, LANES), lambda b: (b, seg)),
                  pl.BlockSpec((S, LANES), lambda b: (b, seg + 1)),
                  pl.BlockSpec((S, LANES), lambda b: (b, seg + 2)),
                  pl.BlockSpec((None, 1, ATT_DH), lambda b: (layer, 0, 0)),
                  pl.BlockSpec((None, 1, LANES), lambda b: (layer, 0, 0))],
        out_specs=[pl.BlockSpec((None, S, ATT_DH), lambda b: (b, 0, 0)),
                   pl.BlockSpec((None, ATT_DH, S), lambda b: (b, 0, 0)),
                   pl.BlockSpec((None, S, LANES), lambda b: (b, 0, 0))],
        out_shape=[jax.ShapeDtypeStruct((B, S, ATT_DH), BF16),
                   jax.ShapeDtypeStruct((B, ATT_DH, S), BF16),
                   jax.ShapeDtypeStruct((B, S, LANES), BF16)],
        compiler_params=_params(1),
        name="prepare_keys",
    )(p_small, p_small, p_small, gk, gik)


def _sparse_attention_kernel(iq_ref, iw_ref, aq_ref, ag_ref, kn_ref, vt_ref, ikn_ref, gq_ref, *rest,
                             topk, first_block):
    o_ref, key_ref, bias_ref, qt_ref, s_ref, p_ref, ot_ref = rest[-7:]
    S = kn_ref.shape[0]
    TQ = iq_ref.shape[0]
    i = pl.program_id(1) + first_block

    iqT = iq_ref[...].astype(F32).T.astype(BF16)
    wT = iw_ref[...].astype(F32).T[IDX_DH:IDX_DH + IDX_HEADS, :] * (IDX_HEADS ** -0.5 * IDX_DH ** -0.5)
    no_rows = jnp.zeros((LANES - IDX_DH, 2 * TQ), BF16)
    score = jnp.zeros((S, TQ), F32)
    for hp in range(IDX_HEADS // 2):
        h0, h1 = 2 * hp, 2 * hp + 1
        q_pair = jnp.concatenate([iqT[h0 * IDX_DH:(h0 + 1) * IDX_DH, :], iqT[h1 * IDX_DH:(h1 + 1) * IDX_DH, :]],
                                 axis=1)
        s = jnp.dot(ikn_ref[...], jnp.concatenate([q_pair, no_rows], axis=0), preferred_element_type=F32)
        score = score + jnp.maximum(s[:, :TQ], 0.0) * wT[h0:h0 + 1, :]
        score = score + jnp.maximum(s[:, TQ:], 0.0) * wT[h1:h1 + 1, :]

    bits = lax.bitcast_convert_type(score, jnp.int32)
    key = bits ^ ((bits >> 31) & 0x7FFFFFFF)
    kpos = lax.broadcasted_iota(jnp.int32, (S, TQ), 0)
    qpos = lax.broadcasted_iota(jnp.int32, (S, TQ), 1) + i * TQ
    key = jnp.where(kpos // CHUNK <= qpos // CHUNK, key, INT_MIN)
    key_ref[...] = key

    def count(pred):
        acc = jnp.zeros((KEY_ROWS, TQ), F32)
        for c in range(S // KEY_ROWS):
            rows = slice(c * KEY_ROWS, (c + 1) * KEY_ROWS)
            kpos = lax.broadcasted_iota(jnp.int32, (KEY_ROWS, TQ), 0) + c * KEY_ROWS
            acc = acc + jnp.where(pred(key_ref[rows, :], kpos), 1.0, 0.0)
        return _reduce_rows(acc, jnp.add, jnp.sum)

    def try_threshold(cand, thr, cnt):
        c = count(lambda key, kpos: key >= cand)
        ok = c >= topk
        return jnp.where(ok, cand, thr), jnp.where(ok, c, cnt)

    thr = jnp.full((1, TQ), INT_MIN, jnp.int32)
    cnt = jnp.full((1, TQ), float(S), F32)
    thr, cnt = try_threshold(jnp.zeros((1, TQ), jnp.int32), thr, cnt)

    def bit_step(it, carry):
        thr, cnt = carry
        return try_threshold(thr | (1 << (30 - it)), thr, cnt)

    thr, cnt = lax.fori_loop(0, 31, bit_step, (thr, cnt))

    def selected_bias(c, selected):
        key = key_ref[c * KEY_ROWS:(c + 1) * KEY_ROWS, :]
        kpos = lax.broadcasted_iota(jnp.int32, (KEY_ROWS, TQ), 0) + c * KEY_ROWS
        finite = (key > KEY_NEG_INF) & (key < KEY_POS_INF)
        b = jnp.where(selected(key, kpos) & finite, 0.0, MASKED_LOGIT)
        bias_ref[c * KEY_ROWS:(c + 1) * KEY_ROWS, :] = b.astype(BF16)

    for c in range(S // KEY_ROWS):
        selected_bias(c, lambda key, kpos: key >= thr)

    tied = jnp.where((cnt > topk) & (thr > KEY_NEG_INF), 1.0, 0.0)

    @pl.when(jnp.max(tied) > 0.0)
    def _break_ties():
        need = topk - count(lambda key, kpos: key > thr)

        def pos_step(it, bound):
            cand = bound | (1 << (S.bit_length() - 1 - it))
            ok = count(lambda key, kpos: (key == thr) & (kpos < cand)) <= need
            return jnp.where(ok, cand, bound)

        bound = lax.fori_loop(0, S.bit_length(), pos_step, jnp.zeros((1, TQ), jnp.int32))
        for c in range(S // KEY_ROWS):
            selected_bias(c, lambda key, kpos: (key > thr) | ((key == thr) & (kpos < bound)))

    aqT = aq_ref[...].astype(F32).T
    gq = gq_ref[...] * (ATT_DH ** -0.5 * LOG2_E)
    for h in range(ATT_HEADS):
        x = aqT[h * ATT_DH:(h + 1) * ATT_DH, :]
        r = lax.rsqrt(_reduce_rows(x * x, jnp.add, jnp.sum) * (1.0 / ATT_DH) + EPS)
        qt_ref[h * ATT_DH:(h + 1) * ATT_DH, :] = (x * r * gq).astype(BF16)

    n_pairs = ATT_HEADS // 2
    blocks = [slice(c * KEY_ROWS, (c + 1) * KEY_ROWS) for c in range(S // KEY_ROWS)]

    eye = (lax.broadcasted_iota(jnp.int32, (TQ, TQ), 0) == lax.broadcasted_iota(jnp.int32, (TQ, TQ), 1))
    eye = jnp.where(eye, 1.0, 0.0).astype(BF16)
    eye_pair = jnp.concatenate([eye, eye], axis=1)

    def logits_stage(hp, rows, q_pair, top):
        s = jnp.dot(jnp.concatenate([kn_ref[rows, :], bias_ref[rows, :]], axis=1),
                    jnp.concatenate([q_pair, eye_pair], axis=0), preferred_element_type=F32)
        s_ref[hp % 2, rows, :] = s
        return jnp.maximum(top, _fold_rows(s, jnp.maximum))

    def probs_stage(hp, rows, top, den):
        p = jnp.exp2(s_ref[hp % 2, rows, :] - top)
        p_ref[hp % 2, rows, :] = p.astype(BF16)
        return den + _fold_rows(p, jnp.add)

    def values_stage(hp, rows, acc):
        return acc + jnp.dot(vt_ref[:, rows], p_ref[hp % 2, rows, :], preferred_element_type=F32)

    def query_pair(hp):
        row0 = hp * 2 * ATT_DH
        return jnp.concatenate([qt_ref[row0:row0 + ATT_DH, :], qt_ref[row0 + ATT_DH:row0 + 2 * ATT_DH, :]], axis=1)

    def finish(hp, acc, den):
        oT = acc / den
        row0 = hp * 2 * ATT_DH
        ot_ref[row0:row0 + ATT_DH, :] = oT[:, :TQ]
        ot_ref[row0 + ATT_DH:row0 + 2 * ATT_DH, :] = oT[:, TQ:]

    neg = jnp.full((8, 2 * TQ), -jnp.inf, F32)
    q_next = query_pair(0)
    top_next = neg
    for rows in blocks:
        top_next = logits_stage(0, rows, q_next, top_next)
    den_prev = None
    for hp in range(n_pairs):
        top = jnp.max(top_next, axis=0, keepdims=True)
        if hp + 1 < n_pairs:
            q_next = query_pair(hp + 1)
            top_next = neg
        den = jnp.zeros((8, 2 * TQ), F32)
        acc = jnp.zeros((ATT_DH, 2 * TQ), F32)
        for rows in blocks:
            if hp + 1 < n_pairs:
                top_next = logits_stage(hp + 1, rows, q_next, top_next)
            den = probs_stage(hp, rows, top, den)
            if hp > 0:
                acc = values_stage(hp - 1, rows, acc)
        if hp > 0:
            finish(hp - 1, acc, den_prev)
        den_prev = jnp.sum(den, axis=0, keepdims=True)
    acc = jnp.zeros((ATT_DH, 2 * TQ), F32)
    for rows in blocks:
        acc = values_stage(n_pairs - 1, rows, acc)
    finish(n_pairs - 1, acc, den_prev)

    o_ref[...] = (ot_ref[...].T * _silu(ag_ref[...].astype(F32))).astype(o_ref.dtype)


def _sparse_attention(p_main, p_small, keys, gq, layer, B, S):
    T = p_main.shape[0]
    W = ATT_HEADS * ATT_DH
    TQ = Q_BLOCK
    nb = S // TQ
    topk = min(INDEX_TOPK, S // 4)
    n_iq = IDX_HEADS * IDX_DH
    seg = n_iq // LANES
    kn, vt, ikn = keys
    group_keys = max(KEY_ROWS, S // 8)
    group_blocks = group_keys // TQ
    out = None
    for first_block in range(0, nb, group_blocks):
        n_keys = (first_block + group_blocks) * TQ
        qrow = lambda b, i, first_block=first_block: b * nb + first_block + i
        in_specs = [pl.BlockSpec((TQ, n_iq), lambda b, i: (qrow(b, i), 0)),
                    pl.BlockSpec((TQ, LANES), lambda b, i: (qrow(b, i), seg + 2)),
                    pl.BlockSpec((TQ, W), lambda b, i: (qrow(b, i), 4)),
                    pl.BlockSpec((TQ, W), lambda b, i: (qrow(b, i), 5)),
                    pl.BlockSpec((None, n_keys, ATT_DH), lambda b, i: (b, 0, 0)),
                    pl.BlockSpec((None, ATT_DH, n_keys), lambda b, i: (b, 0, 0)),
                    pl.BlockSpec((None, n_keys, LANES), lambda b, i: (b, 0, 0)),
                    pl.BlockSpec((None, ATT_DH, 1), lambda b, i: (layer, 0, 0))]
        args = [p_small, p_small, p_main, p_main, kn, vt, ikn, gq]
        aliases = {}
        if out is not None:
            in_specs.append(pl.BlockSpec(memory_space=pl.ANY))
            aliases = {len(args): 0}
            args.append(out)
        out = pl.pallas_call(
            functools.partial(_sparse_attention_kernel, topk=topk, first_block=first_block),
            grid=(B, group_blocks),
            in_specs=in_specs,
            out_specs=pl.BlockSpec((TQ, W), lambda b, i: (qrow(b, i), 0)),
            out_shape=jax.ShapeDtypeStruct((T, W), BF16),
            scratch_shapes=[pltpu.VMEM((n_keys, TQ), jnp.int32),
                            pltpu.VMEM((n_keys, TQ), BF16),
                            pltpu.VMEM((W, TQ), BF16),
                            pltpu.VMEM((2, n_keys, 2 * TQ), F32),
                            pltpu.VMEM((2, n_keys, 2 * TQ), BF16),
                            pltpu.VMEM((W, TQ), F32)],
            input_output_aliases=aliases,
            compiler_params=_params(2),
            name="sparse_attention",
        )(*args)
    return out


def _merge_kernel(ar_ref, aa_ref, wr_ref, wa_ref, ga_ref, gb_ref, m_ref, wrb_ref, wab_ref):
    @pl.when(pl.program_id(1) == 0)
    def _():
        wrb_ref[...] = wr_ref[...].astype(BF16)
        wab_ref[...] = wa_ref[...].astype(BF16)

    u_ret = jnp.dot(ar_ref[...], wrb_ref[...], preferred_element_type=F32)
    u_att = jnp.dot(aa_ref[...], wab_ref[...], preferred_element_type=F32)
    m = _sigmoid(ga_ref[...].astype(F32)) * u_ret + _sigmoid(gb_ref[...].astype(F32)) * u_att
    m_ref[...] = m.astype(m_ref.dtype)


def _merge(a_ret, a_att, w_ret, w_att, p_main, layer):
    T, K = a_ret.shape
    D = w_ret.shape[-1]
    tm = _tile(T, 512)
    tn = _tile(D, 1024)
    gate_a = 6 * (D // tn)
    gate_b = 7 * (D // tn)
    w_spec = pl.BlockSpec((None, K, tn), lambda j, i: (layer, 0, j), pipeline_mode=pl.Buffered(1))
    return pl.pallas_call(
        _merge_kernel,
        grid=(D // tn, T // tm),
        in_specs=[pl.BlockSpec((tm, K), lambda j, i: (i, 0)),
                  pl.BlockSpec((tm, K), lambda j, i: (i, 0)),
                  w_spec, w_spec,
                  pl.BlockSpec((tm, tn), lambda j, i: (i, gate_a + j)),
                  pl.BlockSpec((tm, tn), lambda j, i: (i, gate_b + j))],
        out_specs=pl.BlockSpec((tm, tn), lambda j, i: (i, j)),
        out_shape=jax.ShapeDtypeStruct((T, D), BF16),
        scratch_shapes=[pltpu.VMEM((K, tn), BF16), pltpu.VMEM((K, tn), BF16)],
        compiler_params=_params(2),
        name="merge",
    )(a_ret, a_att, w_ret, w_att, p_main, p_main)


def _output_kernel(m_ref, w_ref, x_ref, *refs):
    wb_ref = refs[-1]

    @pl.when(pl.program_id(0) == 0)
    def _():
        wb_ref[...] = w_ref[...].astype(BF16)

    x = x_ref[...] + jnp.dot(m_ref[...], wb_ref[...], preferred_element_type=F32)
    if len(refs) == 2:
        refs[0][...] = x
    else:
        g_ref, xo_ref, h_ref = refs[:3]
        xo_ref[...] = x
        r = lax.rsqrt(jnp.mean(x * x, axis=-1, keepdims=True) + EPS)
        h_ref[...] = (x * r * g_ref[...]).astype(h_ref.dtype)


def _output(m, w_out, x, norm_g, layer, last):
    T, D = x.shape
    tm = _tile(T, 256)
    row_spec = pl.BlockSpec((tm, D), lambda i: (i, 0))
    w_spec = pl.BlockSpec((None, D, D), lambda i: (layer, 0, 0), pipeline_mode=pl.Buffered(1))
    in_specs = [row_spec, w_spec, row_spec]
    scratch = [pltpu.VMEM((D, D), BF16)]
    if last:
        return pl.pallas_call(
            _output_kernel, grid=(T // tm,), in_specs=in_specs, out_specs=row_spec,
            out_shape=jax.ShapeDtypeStruct((T, D), F32), scratch_shapes=scratch,
            compiler_params=_params(1), name="output_last",
        )(m, w_out, x), None
    return pl.pallas_call(
        _output_kernel, grid=(T // tm,),
        in_specs=in_specs + [pl.BlockSpec((None, 1, D), lambda i: (layer + 1, 0, 0))],
        out_specs=[row_spec, row_spec],
        out_shape=[jax.ShapeDtypeStruct((T, D), F32), jax.ShapeDtypeStruct((T, D), BF16)],
        scratch_shapes=scratch, compiler_params=_params(1), name="output",
    )(m, w_out, x, norm_g)


def kernel(x, norm_g, w_in, ret_out_g, att_q_g, att_k_g, idx_k_g, w_branch_ret, w_branch_att, w_out):
    B, S, D = x.shape
    depth = w_in.shape[0]
    T = B * S
    assert w_in.shape[-1] == N_IN and S % KEY_ROWS == 0
    norm_g3 = norm_g.reshape(depth, 1, D)
    ret_g4 = ret_out_g.reshape(depth, RET_HEADS, 1, RET_DV)
    gq = att_q_g.reshape(depth, ATT_DH, 1)
    gk = att_k_g.reshape(depth, 1, ATT_DH)
    gik = jnp.pad(idx_k_g, ((0, 0), (0, LANES - IDX_DH))).reshape(depth, 1, LANES)
    tables = _retention_tables(S, _tile(S, 256))
    w_t = jnp.swapaxes(w_in, 1, 2)

    xf = x.reshape(T, D)
    h = _rmsnorm(xf, norm_g3, 0)
    for layer in range(depth):
        p_main = _project_main(h, w_t, layer)
        p_small = _project_small(h, w_t, layer)
        a_ret = _retention(p_main, tables, ret_g4, layer, B, S)
        keys = _prepare_keys(p_small, gk, gik, layer, B, S)
        a_att = _sparse_attention(p_main, p_small, keys, gq, layer, B, S)
        m = _merge(a_ret, a_att, w_branch_ret, w_branch_att, p_main, layer)
        xf, h = _output(m, w_out, xf, norm_g3, layer, last=layer == depth - 1)
    return xf.reshape(B, S, D)
```

```python
import functools
import math

import jax
import jax.numpy as jnp
from jax import lax
from jax.experimental import pallas as pl
from jax.experimental.pallas import tpu as pltpu

F32 = jnp.float32
BF16 = jnp.bfloat16

CHUNK = 64
EPS = 1e-6
RET_HEADS = 8
RET_DK = 256
RET_DV = 256
ROT_BASE = 10000.0
ATT_HEADS = 16
ATT_DH = 128
IDX_HEADS = 16
IDX_DH = 64
INDEX_TOPK = 256
Q_BLOCK = 128
LANES = 128
KEY_ROWS = 256

INT_MIN = -(2 ** 31)
KEY_NEG_INF = -2139095041
KEY_POS_INF = 2139095040
MASKED_LOGIT = -1e30
LOG2_E = 1.4426950408889634

VMEM_LIMIT_BYTES = 56 * 1024 * 1024


def _params(n_axes):
    return pltpu.CompilerParams(dimension_semantics=("arbitrary",) * n_axes,
                                vmem_limit_bytes=VMEM_LIMIT_BYTES)


def _tile(n, pref):
    t = min(n, pref)
    assert n % t == 0, (n, t)
    return t


def _sigmoid(g):
    return 0.5 * jnp.tanh(0.5 * g) + 0.5


def _silu(g):
    half = 0.5 * g
    return half * jnp.tanh(half) + half


def _fold_rows(x, op):
    SUBLANES = 8
    while x.shape[0] > SUBLANES and x.shape[0] % (2 * SUBLANES) == 0:
        half = x.shape[0] // 2
        x = op(x[:half], x[half:])
    return x


def _reduce_rows(x, op, finish):
    return finish(_fold_rows(x, op), axis=0, keepdims=True)


def _rmsnorm_kernel(x_ref, g_ref, h_ref):
    x = x_ref[...]
    r = lax.rsqrt(jnp.mean(x * x, axis=-1, keepdims=True) + EPS)
    h_ref[...] = (x * r * g_ref[...]).astype(h_ref.dtype)


def _rmsnorm(x, norm_g, layer):
    T, D = x.shape
    tm = _tile(T, 512)
    return pl.pallas_call(
        _rmsnorm_kernel,
        grid=(T // tm,),
        in_specs=[pl.BlockSpec((tm, D), lambda i: (i, 0)),
                  pl.BlockSpec((None, 1, D), lambda i: (layer, 0, 0))],
        out_specs=pl.BlockSpec((tm, D), lambda i: (i, 0)),
        out_shape=jax.ShapeDtypeStruct((T, D), BF16),
        compiler_params=_params(1),
        name="rmsnorm",
    )(x, norm_g)


OFF_AQ = 4 * RET_HEADS * RET_DK
OFF_AK = OFF_AQ + ATT_HEADS * ATT_DH
OFF_AV = OFF_AK + ATT_DH
OFF_AG = OFF_AV + ATT_DH
OFF_IQ = OFF_AG + ATT_HEADS * ATT_DH
OFF_IK = OFF_IQ + IDX_HEADS * IDX_DH
OFF_IW = OFF_IK + IDX_DH
OFF_GA = OFF_IW + IDX_HEADS
N_IN = OFF_GA + 2 * RET_HEADS * RET_DK


def _project_kernel(a_ref, *refs):
    *w_refs, o_ref, wb_ref = refs
    K = wb_ref.shape[0]
    STRIP = 512

    @pl.when(pl.program_id(1) == 0)
    def _():
        for k0 in range(0, K, STRIP):
            parts = [r[0, :, k0:k0 + STRIP] for r in w_refs]
            w = parts[0] if len(parts) == 1 else jnp.concatenate(parts, axis=0)
            wb_ref[k0:k0 + STRIP, :] = w.T.astype(BF16)

    o_ref[...] = jnp.dot(a_ref[...], wb_ref[...], preferred_element_type=F32).astype(o_ref.dtype)


def _project(h, w_t, windows, n_tiles, name):
    T, K = h.shape
    tm = _tile(T, 1024)
    tn = sum(rows for rows, _ in windows)
    specs = [pl.BlockSpec((pl.Element(1), pl.Element(rows), pl.Element(K)), index_map)
             for rows, index_map in windows]
    return pl.pallas_call(
        _project_kernel,
        grid=(n_tiles, T // tm),
        in_specs=[pl.BlockSpec((tm, K), lambda j, i: (i, 0))] + specs,
        out_specs=pl.BlockSpec((tm, tn), lambda j, i: (i, j)),
        out_shape=jax.ShapeDtypeStruct((T, n_tiles * tn), BF16),
        scratch_shapes=[pltpu.VMEM((K, tn), BF16)],
        compiler_params=_params(2),
        name=name,
    )(h, *([w_t] * len(specs)))


def _project_main(h, w_t, layer):
    tn = 1024
    SUB = 8
    runs = ((0, OFF_AK), (OFF_AG, OFF_IQ), (OFF_GA, N_IN))
    assert all(a % SUB == 0 and (b - a) % tn == 0 for a, b in runs)
    tiles_before = [sum((b - a) // tn for a, b in runs[:r]) for r in range(len(runs) + 1)]

    def first_row(j):
        row = j * (tn // SUB)
        for r in range(1, len(runs)):
            gap = runs[r][0] - runs[r - 1][1]
            row = row + jnp.where(j >= tiles_before[r], gap // SUB, 0)
        return row * SUB

    return _project(h, w_t, [(tn, lambda j, i: (layer, first_row(j), 0))], tiles_before[-1], "project_main")


def _project_small(h, w_t, layer):
    windows = [(IDX_HEADS * IDX_DH, lambda j, i: (layer, OFF_IQ, 0)),
               (2 * ATT_DH, lambda j, i: (layer, OFF_AK, 0)),
               (LANES, lambda j, i: (layer, OFF_IK, 0))]
    return _project(h, w_t, windows, 1, "project_small")


def _retention_tables(S, R):
    pos = jnp.arange(S, dtype=F32)
    inv = 1.0 / (ROT_BASE ** jnp.linspace(0.0, 1.0, RET_DK // 2, dtype=F32))
    ang = pos[:, None] * inv[None, :]
    log_g = jnp.log(1.0 - 2.0 ** (-5.0 - jnp.arange(RET_HEADS, dtype=F32)))[:, None, None]
    idx = jnp.arange(R, dtype=F32)
    chunk = jnp.arange(R) // CHUNK
    visible = chunk[None, :] <= chunk[:, None]
    k_scale = RET_DK ** -0.5
    assert k_scale == 2.0 ** round(math.log2(k_scale))
    dm = jnp.where(visible[None], jnp.exp(jnp.abs(idx[:, None] - idx[None, :])[None] * log_g), 0.0) * k_scale
    ones = jnp.ones((1, 1, RET_DV), F32)
    qd = jnp.exp((idx[None, :, None] + 1.0) * log_g) * ones
    kd = jnp.exp((R - 1.0 - idx)[None, :, None] * log_g) * ones * k_scale
    cd = jnp.exp(float(R) * log_g) * ones
    return jnp.cos(ang), jnp.sin(ang), dm, qd, kd, cd


def _retention_kernel(q_ref, k_ref, v_ref, g_ref, cos_ref, sin_ref, dm_ref, qd_ref, kd_ref, cd_ref,
                      og_ref, o_ref, state_ref):
    @pl.when(pl.program_id(1) == 0)
    def _():
        state_ref[...] = jnp.zeros_like(state_ref)

    cos = cos_ref[...]
    sin = sin_ref[...]
    half = RET_DK // 2

    def rotated(ref, h):
        x1 = ref[:, h * RET_DK:h * RET_DK + half].astype(F32)
        x2 = ref[:, h * RET_DK + half:(h + 1) * RET_DK].astype(F32)
        return jnp.concatenate([x1 * cos - x2 * sin, x2 * cos + x1 * sin], axis=1)

    for h in range(RET_HEADS):
        cols = slice(h * RET_DV, (h + 1) * RET_DV)
        qr = rotated(q_ref, h)
        kr = rotated(k_ref, h)
        qb = qr.astype(BF16)
        kb = kr.astype(BF16)
        v = v_ref[:, cols]
        s = lax.dot_general(qb, kb, (((1,), (1,)), ((), ())), preferred_element_type=F32) * dm_ref[h]
        o = jnp.dot(s.astype(BF16), v, preferred_element_type=F32)
        state = state_ref[h]
        o = o + jnp.dot(qb, state.astype(BF16), preferred_element_type=F32) * qd_ref[h]
        kdec = (kr * kd_ref[h]).astype(BF16)
        state_ref[h] = state * cd_ref[h] + lax.dot_general(
            kdec, v, (((0,), (0,)), ((), ())), preferred_element_type=F32)
        r = lax.rsqrt(jnp.mean(o * o, axis=-1, keepdims=True) + EPS)
        g = g_ref[:, cols].astype(F32)
        o_ref[:, cols] = (o * r * og_ref[h] * _silu(g)).astype(o_ref.dtype)


def _retention(p_main, tables, ret_out_g, layer, B, S):
    T = p_main.shape[0]
    W = RET_HEADS * RET_DV
    R = tables[2].shape[1]
    nr = S // R
    cos, sin, dm, qd, kd, cd = tables
    row = lambda b, i: b * nr + i
    full3 = lambda a: pl.BlockSpec(a.shape, lambda b, i: (0, 0, 0))
    return pl.pallas_call(
        _retention_kernel,
        grid=(B, nr),
        in_specs=[pl.BlockSpec((R, W), lambda b, i: (row(b, i), 0)),
                  pl.BlockSpec((R, W), lambda b, i: (row(b, i), 1)),
                  pl.BlockSpec((R, W), lambda b, i: (row(b, i), 2)),
                  pl.BlockSpec((R, W), lambda b, i: (row(b, i), 3)),
                  pl.BlockSpec((R, RET_DK // 2), lambda b, i: (i, 0)),
                  pl.BlockSpec((R, RET_DK // 2), lambda b, i: (i, 0)),
                  full3(dm), full3(qd), full3(kd), full3(cd),
                  pl.BlockSpec((None, RET_HEADS, 1, RET_DV), lambda b, i: (layer, 0, 0, 0))],
        out_specs=pl.BlockSpec((R, W), lambda b, i: (row(b, i), 0)),
        out_shape=jax.ShapeDtypeStruct((T, W), BF16),
        scratch_shapes=[pltpu.VMEM((RET_HEADS, RET_DK, RET_DV), F32)],
        compiler_params=_params(2),
        name="retention",
    )(p_main, p_main, p_main, p_main, cos, sin, dm, qd, kd, cd, ret_out_g)


def _prepare_keys_kernel(ak_ref, av_ref, ik_ref, gk_ref, gik_ref, kn_ref, vt_ref, ikn_ref):
    ak = ak_ref[...].astype(F32)
    r = lax.rsqrt(jnp.mean(ak * ak, axis=-1, keepdims=True) + EPS)
    kn_ref[...] = (ak * r * gk_ref[...]).astype(BF16)
    vt_ref[...] = av_ref[...].astype(F32).T.astype(BF16)
    ik = ik_ref[...].astype(F32)
    ik = jnp.where(lax.broadcasted_iota(jnp.int32, ik.shape, 1) < IDX_DH, ik, 0.0)
    r = lax.rsqrt(jnp.sum(ik * ik, axis=-1, keepdims=True) * (1.0 / IDX_DH) + EPS)
    ikn_ref[...] = (ik * r * gik_ref[...]).astype(BF16)


def _prepare_keys(p_small, gk, gik, layer, B, S):
    seg = IDX_HEADS * IDX_DH // LANES
    return pl.pallas_call(
        _prepare_keys_kernel,
        grid=(B,),
        in_specs=[pl.BlockSpec((S, LANES), lambda b: (b, seg)),
                  pl.BlockSpec((S, LANES), lambda b: (b, seg + 1)),
                  pl.BlockSpec((S, LANES), lambda b: (b, seg + 2)),
                  pl.BlockSpec((None, 1, ATT_DH), lambda b: (layer, 0, 0)),
                  pl.BlockSpec((None, 1, LANES), lambda b: (layer, 0, 0))],
        out_specs=[pl.BlockSpec((None, S, ATT_DH), lambda b: (b, 0, 0)),
                   pl.BlockSpec((None, ATT_DH, S), lambda b: (b, 0, 0)),
                   pl.BlockSpec((None, S, LANES), lambda b: (b, 0, 0))],
        out_shape=[jax.ShapeDtypeStruct((B, S, ATT_DH), BF16),
                   jax.ShapeDtypeStruct((B, ATT_DH, S), BF16),
                   jax.ShapeDtypeStruct((B, S, LANES), BF16)],
        compiler_params=_params(1),
        name="prepare_keys",
    )(p_small, p_small, p_small, gk, gik)


def _sparse_attention_kernel(iq_ref, iw_ref, aq_ref, ag_ref, kn_ref, vt_ref, ikn_ref, gq_ref, *rest,
                             topk, first_block):
    o_ref, key_ref, bias_ref, qt_ref, s_ref, p_ref, ot_ref = rest[-7:]
    S = kn_ref.shape[0]
    TQ = Q_BLOCK
    n_sub = iq_ref.shape[0] // TQ
    n_bits = 31
    blocks = [slice(c * KEY_ROWS, (c + 1) * KEY_ROWS) for c in range(S // KEY_ROWS)]

    def scores(sub):
        q_rows = slice(sub * TQ, (sub + 1) * TQ)
        iqT = iq_ref[q_rows, :].astype(F32).T.astype(BF16)
        wT = iw_ref[q_rows, :].astype(F32).T[IDX_DH:IDX_DH + IDX_HEADS, :] * (IDX_HEADS ** -0.5 * IDX_DH ** -0.5)
        no_rows = jnp.zeros((LANES - IDX_DH, 2 * TQ), BF16)
        score = jnp.zeros((S, TQ), F32)
        for hp in range(IDX_HEADS // 2):
            h0, h1 = 2 * hp, 2 * hp + 1
            q_pair = jnp.concatenate([iqT[h0 * IDX_DH:(h0 + 1) * IDX_DH, :],
                                      iqT[h1 * IDX_DH:(h1 + 1) * IDX_DH, :]], axis=1)
            s = jnp.dot(ikn_ref[...], jnp.concatenate([q_pair, no_rows], axis=0), preferred_element_type=F32)
            score = score + jnp.maximum(s[:, :TQ], 0.0) * wT[h0:h0 + 1, :]
            score = score + jnp.maximum(s[:, TQ:], 0.0) * wT[h1:h1 + 1, :]
        bits = lax.bitcast_convert_type(score, jnp.int32)
        key = bits ^ ((bits >> 31) & 0x7FFFFFFF)
        first_query = (first_block + pl.program_id(1) * n_sub + sub) * TQ
        kpos = lax.broadcasted_iota(jnp.int32, (S, TQ), 0)
        qpos = lax.broadcasted_iota(jnp.int32, (S, TQ), 1) + first_query
        key_ref[sub] = jnp.where(kpos // CHUNK <= qpos // CHUNK, key, INT_MIN)

    def count(sub, pred):
        acc = jnp.zeros((KEY_ROWS, TQ), F32)
        for c, rows in enumerate(blocks):
            kpos = lax.broadcasted_iota(jnp.int32, (KEY_ROWS, TQ), 0) + c * KEY_ROWS
            acc = acc + jnp.where(pred(key_ref[sub, rows, :], kpos), 1.0, 0.0)
        return _reduce_rows(acc, jnp.add, jnp.sum)

    def try_threshold(sub, cand, thr, cnt):
        c = count(sub, lambda key, kpos: key >= cand)
        ok = c >= topk
        return jnp.where(ok, cand, thr), jnp.where(ok, c, cnt)

    def sign_step(sub):
        thr = jnp.full((1, TQ), INT_MIN, jnp.int32)
        cnt = jnp.full((1, TQ), float(S), F32)
        return try_threshold(sub, jnp.zeros((1, TQ), jnp.int32), thr, cnt)

    def bit_step(sub, it, thr, cnt):
        return try_threshold(sub, thr | (1 << (n_bits - 1 - it)), thr, cnt)

    def selected_bias(sub, selected):
        for c, rows in enumerate(blocks):
            key = key_ref[sub, rows, :]
            kpos = lax.broadcasted_iota(jnp.int32, (KEY_ROWS, TQ), 0) + c * KEY_ROWS
            finite = (key > KEY_NEG_INF) & (key < KEY_POS_INF)
            b = jnp.where(selected(key, kpos) & finite, 0.0, MASKED_LOGIT)
            bias_ref[sub, rows, :] = b.astype(BF16)

    def select(sub, thr, cnt):
        selected_bias(sub, lambda key, kpos: key >= thr)
        tied = jnp.where((cnt > topk) & (thr > KEY_NEG_INF), 1.0, 0.0)

        @pl.when(jnp.max(tied) > 0.0)
        def _break_ties():
            need = topk - count(sub, lambda key, kpos: key > thr)

            def pos_step(it, bound):
                cand = bound | (1 << (S.bit_length() - 1 - it))
                ok = count(sub, lambda key, kpos: (key == thr) & (kpos < cand)) <= need
                return jnp.where(ok, cand, bound)

            bound = lax.fori_loop(0, S.bit_length(), pos_step, jnp.zeros((1, TQ), jnp.int32))
            selected_bias(sub, lambda key, kpos: (key > thr) | ((key == thr) & (kpos < bound)))

    def normalise_queries(sub):
        aqT = aq_ref[sub * TQ:(sub + 1) * TQ, :].astype(F32).T
        gq = gq_ref[...] * (ATT_DH ** -0.5 * LOG2_E)
        for h in range(ATT_HEADS):
            x = aqT[h * ATT_DH:(h + 1) * ATT_DH, :]
            r = lax.rsqrt(_reduce_rows(x * x, jnp.add, jnp.sum) * (1.0 / ATT_DH) + EPS)
            qt_ref[sub, h * ATT_DH:(h + 1) * ATT_DH, :] = (x * r * gq).astype(BF16)

    n_pairs = ATT_HEADS // 2
    eye = (lax.broadcasted_iota(jnp.int32, (TQ, TQ), 0) == lax.broadcasted_iota(jnp.int32, (TQ, TQ), 1))
    eye = jnp.where(eye, 1.0, 0.0).astype(BF16)
    eye_pair = jnp.concatenate([eye, eye], axis=1)

    def attention(sub):
        def logits_stage(hp, rows, q_pair, top):
            s = jnp.dot(jnp.concatenate([kn_ref[rows, :], bias_ref[sub, rows, :]], axis=1),
                        jnp.concatenate([q_pair, eye_pair], axis=0), preferred_element_type=F32)
            s_ref[hp % 2, rows, :] = s
            return jnp.maximum(top, _fold_rows(s, jnp.maximum))

        def probs_stage(hp, rows, top, den):
            p = jnp.exp2(s_ref[hp % 2, rows, :] - top)
            p_ref[hp % 2, rows, :] = p.astype(BF16)
            return den + _fold_rows(p, jnp.add)

        def values_stage(hp, rows, acc):
            return acc + jnp.dot(vt_ref[:, rows], p_ref[hp % 2, rows, :], preferred_element_type=F32)

        def query_pair(hp):
            row0 = hp * 2 * ATT_DH
            return jnp.concatenate([qt_ref[sub, row0:row0 + ATT_DH, :],
                                    qt_ref[sub, row0 + ATT_DH:row0 + 2 * ATT_DH, :]], axis=1)

        def finish(hp, acc, den):
            oT = acc / den
            row0 = hp * 2 * ATT_DH
            ot_ref[sub, row0:row0 + ATT_DH, :] = oT[:, :TQ]
            ot_ref[sub, row0 + ATT_DH:row0 + 2 * ATT_DH, :] = oT[:, TQ:]

        neg = jnp.full((8, 2 * TQ), -jnp.inf, F32)
        q_next = query_pair(0)
        top_next = neg
        for rows in blocks:
            top_next = logits_stage(0, rows, q_next, top_next)
        den_prev = None
        for hp in range(n_pairs):
            top = jnp.max(top_next, axis=0, keepdims=True)
            if hp + 1 < n_pairs:
                q_next = query_pair(hp + 1)
                top_next = neg
            den = jnp.zeros((8, 2 * TQ), F32)
            acc = jnp.zeros((ATT_DH, 2 * TQ), F32)
            for rows in blocks:
                if hp + 1 < n_pairs:
                    top_next = logits_stage(hp + 1, rows, q_next, top_next)
                den = probs_stage(hp, rows, top, den)
                if hp > 0:
                    acc = values_stage(hp - 1, rows, acc)
            if hp > 0:
                finish(hp - 1, acc, den_prev)
            den_prev = jnp.sum(den, axis=0, keepdims=True)
        acc = jnp.zeros((ATT_DH, 2 * TQ), F32)
        for rows in blocks:
            acc = values_stage(n_pairs - 1, rows, acc)
        finish(n_pairs - 1, acc, den_prev)

    for sub in range(n_sub):
        scores(sub)
    found = tuple(sign_step(sub) for sub in range(n_sub))
    found = lax.fori_loop(
        0, n_bits, lambda it, found: tuple(bit_step(sub, it, *found[sub]) for sub in range(n_sub)), found)
    for sub in range(n_sub):
        select(sub, *found[sub])
        normalise_queries(sub)
    for sub in range(n_sub):
        attention(sub)

    for sub in range(n_sub):
        q_rows = slice(sub * TQ, (sub + 1) * TQ)
        o_ref[q_rows, :] = (ot_ref[sub].T * _silu(ag_ref[q_rows, :].astype(F32))).astype(o_ref.dtype)


def _sparse_attention(p_main, p_small, keys, gq, layer, B, S):
    T = p_main.shape[0]
    W = ATT_HEADS * ATT_DH
    TQ = Q_BLOCK
    nb = S // TQ
    topk = min(INDEX_TOPK, S // 4)
    n_iq = IDX_HEADS * IDX_DH
    seg = n_iq // LANES
    kn, vt, ikn = keys
    group_keys = max(KEY_ROWS, S // 8)
    group_blocks = group_keys // TQ
    n_sub = 2
    assert group_blocks % n_sub == 0
    rows = n_sub * TQ
    out = None
    for first_block in range(0, nb, group_blocks):
        n_keys = (first_block + group_blocks) * TQ
        qrow = lambda b, i, first_block=first_block: (b * nb + first_block) // n_sub + i
        in_specs = [pl.BlockSpec((rows, n_iq), lambda b, i: (qrow(b, i), 0)),
                    pl.BlockSpec((rows, LANES), lambda b, i: (qrow(b, i), seg + 2)),
                    pl.BlockSpec((rows, W), lambda b, i: (qrow(b, i), 4)),
                    pl.BlockSpec((rows, W), lambda b, i: (qrow(b, i), 5)),
                    pl.BlockSpec((None, n_keys, ATT_DH), lambda b, i: (b, 0, 0)),
                    pl.BlockSpec((None, ATT_DH, n_keys), lambda b, i: (b, 0, 0)),
                    pl.BlockSpec((None, n_keys, LANES), lambda b, i: (b, 0, 0)),
                    pl.BlockSpec((None, ATT_DH, 1), lambda b, i: (layer, 0, 0))]
        args = [p_small, p_small, p_main, p_main, kn, vt, ikn, gq]
        aliases = {}
        if out is not None:
            in_specs.append(pl.BlockSpec(memory_space=pl.ANY))
            aliases = {len(args): 0}
            args.append(out)
        out = pl.pallas_call(
            functools.partial(_sparse_attention_kernel, topk=topk, first_block=first_block),
            grid=(B, group_blocks // n_sub),
            in_specs=in_specs,
            out_specs=pl.BlockSpec((rows, W), lambda b, i: (qrow(b, i), 0)),
            out_shape=jax.ShapeDtypeStruct((T, W), BF16),
            scratch_shapes=[pltpu.VMEM((n_sub, n_keys, TQ), jnp.int32),
                            pltpu.VMEM((n_sub, n_keys, TQ), BF16),
                            pltpu.VMEM((n_sub, W, TQ), BF16),
                            pltpu.VMEM((2, n_keys, 2 * TQ), F32),
                            pltpu.VMEM((2, n_keys, 2 * TQ), BF16),
                            pltpu.VMEM((n_sub, W, TQ), F32)],
            input_output_aliases=aliases,
            compiler_params=_params(2),
            name="sparse_attention",
        )(*args)
    return out


def _merge_kernel(ar_ref, aa_ref, wr_ref, wa_ref, ga_ref, gb_ref, m_ref, wrb_ref, wab_ref):
    @pl.when(pl.program_id(1) == 0)
    def _():
        wrb_ref[...] = wr_ref[...].astype(BF16)
        wab_ref[...] = wa_ref[...].astype(BF16)

    u_ret = jnp.dot(ar_ref[...], wrb_ref[...], preferred_element_type=F32)
    u_att = jnp.dot(aa_ref[...], wab_ref[...], preferred_element_type=F32)
    m = _sigmoid(ga_ref[...].astype(F32)) * u_ret + _sigmoid(gb_ref[...].astype(F32)) * u_att
    m_ref[...] = m.astype(m_ref.dtype)


def _merge(a_ret, a_att, w_ret, w_att, p_main, layer):
    T, K = a_ret.shape
    D = w_ret.shape[-1]
    tm = _tile(T, 512)
    tn = _tile(D, 1024)
    gate_a = 6 * (D // tn)
    gate_b = 7 * (D // tn)
    w_spec = pl.BlockSpec((None, K, tn), lambda j, i: (layer, 0, j), pipeline_mode=pl.Buffered(1))
    return pl.pallas_call(
        _merge_kernel,
        grid=(D // tn, T // tm),
        in_specs=[pl.BlockSpec((tm, K), lambda j, i: (i, 0)),
                  pl.BlockSpec((tm, K), lambda j, i: (i, 0)),
                  w_spec, w_spec,
                  pl.BlockSpec((tm, tn), lambda j, i: (i, gate_a + j)),
                  pl.BlockSpec((tm, tn), lambda j, i: (i, gate_b + j))],
        out_specs=pl.BlockSpec((tm, tn), lambda j, i: (i, j)),
        out_shape=jax.ShapeDtypeStruct((T, D), BF16),
        scratch_shapes=[pltpu.VMEM((K, tn), BF16), pltpu.VMEM((K, tn), BF16)],
        compiler_params=_params(2),
        name="merge",
    )(a_ret, a_att, w_ret, w_att, p_main, p_main)


def _output_kernel(m_ref, w_ref, x_ref, *refs):
    wb_ref = refs[-1]

    @pl.when(pl.program_id(0) == 0)
    def _():
        wb_ref[...] = w_ref[...].astype(BF16)

    x = x_ref[...] + jnp.dot(m_ref[...], wb_ref[...], preferred_element_type=F32)
    if len(refs) == 2:
        refs[0][...] = x
    else:
        g_ref, xo_ref, h_ref = refs[:3]
        xo_ref[...] = x
        r = lax.rsqrt(jnp.mean(x * x, axis=-1, keepdims=True) + EPS)
        h_ref[...] = (x * r * g_ref[...]).astype(h_ref.dtype)


def _output(m, w_out, x, norm_g, layer, last):
    T, D = x.shape
    tm = _tile(T, 256)
    row_spec = pl.BlockSpec((tm, D), lambda i: (i, 0))
    w_spec = pl.BlockSpec((None, D, D), lambda i: (layer, 0, 0), pipeline_mode=pl.Buffered(1))
    in_specs = [row_spec, w_spec, row_spec]
    scratch = [pltpu.VMEM((D, D), BF16)]
    if last:
        return pl.pallas_call(
            _output_kernel, grid=(T // tm,), in_specs=in_specs, out_specs=row_spec,
            out_shape=jax.ShapeDtypeStruct((T, D), F32), scratch_shapes=scratch,
            compiler_params=_params(1), name="output_last",
        )(m, w_out, x), None
    return pl.pallas_call(
        _output_kernel, grid=(T // tm,),
        in_specs=in_specs + [pl.BlockSpec((None, 1, D), lambda i: (layer + 1, 0, 0))],
        out_specs=[row_spec, row_spec],
        out_shape=[jax.ShapeDtypeStruct((T, D), F32), jax.ShapeDtypeStruct((T, D), BF16)],
        scratch_shapes=scratch, compiler_params=_params(1), name="output",
    )(m, w_out, x, norm_g)


def kernel(x, norm_g, w_in, ret_out_g, att_q_g, att_k_g, idx_k_g, w_branch_ret, w_branch_att, w_out):
    B, S, D = x.shape
    depth = w_in.shape[0]
    T = B * S
    assert w_in.shape[-1] == N_IN and S % KEY_ROWS == 0
    norm_g3 = norm_g.reshape(depth, 1, D)
    ret_g4 = ret_out_g.reshape(depth, RET_HEADS, 1, RET_DV)
    gq = att_q_g.reshape(depth, ATT_DH, 1)
    gk = att_k_g.reshape(depth, 1, ATT_DH)
    gik = jnp.pad(idx_k_g, ((0, 0), (0, LANES - IDX_DH))).reshape(depth, 1, LANES)
    tables = _retention_tables(S, _tile(S, 256))
    w_t = jnp.swapaxes(w_in, 1, 2)

    xf = x.reshape(T, D)
    h = _rmsnorm(xf, norm_g3, 0)
    for layer in range(depth):
        p_main = _project_main(h, w_t, layer)
        p_small = _project_small(h, w_t, layer)
        a_ret = _retention(p_main, tables, ret_g4, layer, B, S)
        keys = _prepare_keys(p_small, gk, gik, layer, B, S)
        a_att = _sparse_attention(p_main, p_small, keys, gq, layer, B, S)
        m = _merge(a_ret, a_att, w_branch_ret, w_branch_att, p_main, layer)
        xf, h = _output(m, w_out, xf, norm_g3, layer, last=layer == depth - 1)
    return xf.reshape(B, S, D)
```

```python
import functools
import math

import jax
import jax.numpy as jnp
from jax import lax
from jax.experimental import pallas as pl
from jax.experimental.pallas import tpu as pltpu

F32 = jnp.float32
BF16 = jnp.bfloat16

CHUNK = 64
EPS = 1e-6
RET_HEADS = 8
RET_DK = 256
RET_DV = 256
ROT_BASE = 10000.0
ATT_HEADS = 16
ATT_DH = 128
IDX_HEADS = 16
IDX_DH = 64
INDEX_TOPK = 256
Q_BLOCK = 128
LANES = 128
KEY_ROWS = 256

INT_MIN = -(2 ** 31)
KEY_NEG_INF = -2139095041
KEY_POS_INF = 2139095040
MASKED_LOGIT = -1e30
LOG2_E = 1.4426950408889634

VMEM_LIMIT_BYTES = 56 * 1024 * 1024


def _params(n_axes):
    return pltpu.CompilerParams(dimension_semantics=("arbitrary",) * n_axes,
                                vmem_limit_bytes=VMEM_LIMIT_BYTES)


def _tile(n, pref):
    t = min(n, pref)
    assert n % t == 0, (n, t)
    return t


def _sigmoid(g):
    return 0.5 * jnp.tanh(0.5 * g) + 0.5


def _silu(g):
    half = 0.5 * g
    return half * jnp.tanh(half) + half


def _fold_rows(x, op):
    SUBLANES = 8
    while x.shape[0] > SUBLANES and x.shape[0] % (2 * SUBLANES) == 0:
        half = x.shape[0] // 2
        x = op(x[:half], x[half:])
    return x


def _reduce_rows(x, op, finish):
    return finish(_fold_rows(x, op), axis=0, keepdims=True)


def _rmsnorm_kernel(x_ref, g_ref, h_ref):
    x = x_ref[...]
    r = lax.rsqrt(jnp.mean(x * x, axis=-1, keepdims=True) + EPS)
    h_ref[...] = (x * r * g_ref[...]).astype(h_ref.dtype)


def _rmsnorm(x, norm_g, layer):
    T, D = x.shape
    tm = _tile(T, 512)
    return pl.pallas_call(
        _rmsnorm_kernel,
        grid=(T // tm,),
        in_specs=[pl.BlockSpec((tm, D), lambda i: (i, 0)),
                  pl.BlockSpec((None, 1, D), lambda i: (layer, 0, 0))],
        out_specs=pl.BlockSpec((tm, D), lambda i: (i, 0)),
        out_shape=jax.ShapeDtypeStruct((T, D), BF16),
        compiler_params=_params(1),
        name="rmsnorm",
    )(x, norm_g)


OFF_AQ = 4 * RET_HEADS * RET_DK
OFF_AK = OFF_AQ + ATT_HEADS * ATT_DH
OFF_AV = OFF_AK + ATT_DH
OFF_AG = OFF_AV + ATT_DH
OFF_IQ = OFF_AG + ATT_HEADS * ATT_DH
OFF_IK = OFF_IQ + IDX_HEADS * IDX_DH
OFF_IW = OFF_IK + IDX_DH
OFF_GA = OFF_IW + IDX_HEADS
N_IN = OFF_GA + 2 * RET_HEADS * RET_DK


def _project_kernel(a_ref, *refs):
    *w_refs, o_ref, wb_ref = refs
    K = wb_ref.shape[0]
    STRIP = 512

    @pl.when(pl.program_id(1) == 0)
    def _():
        for k0 in range(0, K, STRIP):
            parts = [r[0, :, k0:k0 + STRIP] for r in w_refs]
            w = parts[0] if len(parts) == 1 else jnp.concatenate(parts, axis=0)
            wb_ref[k0:k0 + STRIP, :] = w.T.astype(BF16)

    o_ref[...] = jnp.dot(a_ref[...], wb_ref[...], preferred_element_type=F32).astype(o_ref.dtype)


def _project(h, w_t, windows, n_tiles, name):
    T, K = h.shape
    tm = _tile(T, 1024)
    tn = sum(rows for rows, _ in windows)
    specs = [pl.BlockSpec((pl.Element(1), pl.Element(rows), pl.Element(K)), index_map)
             for rows, index_map in windows]
    return pl.pallas_call(
        _project_kernel,
        grid=(n_tiles, T // tm),
        in_specs=[pl.BlockSpec((tm, K), lambda j, i: (i, 0))] + specs,
        out_specs=pl.BlockSpec((tm, tn), lambda j, i: (i, j)),
        out_shape=jax.ShapeDtypeStruct((T, n_tiles * tn), BF16),
        scratch_shapes=[pltpu.VMEM((K, tn), BF16)],
        compiler_params=_params(2),
        name=name,
    )(h, *([w_t] * len(specs)))


def _project_main(h, w_t, layer):
    tn = 1024
    SUB = 8
    runs = ((0, OFF_AK), (OFF_AG, OFF_IQ), (OFF_GA, N_IN))
    assert all(a % SUB == 0 and (b - a) % tn == 0 for a, b in runs)
    tiles_before = [sum((b - a) // tn for a, b in runs[:r]) for r in range(len(runs) + 1)]

    def first_row(j):
        row = j * (tn // SUB)
        for r in range(1, len(runs)):
            gap = runs[r][0] - runs[r - 1][1]
            row = row + jnp.where(j >= tiles_before[r], gap // SUB, 0)
        return row * SUB

    return _project(h, w_t, [(tn, lambda j, i: (layer, first_row(j), 0))], tiles_before[-1], "project_main")


def _project_small(h, w_t, layer):
    windows = [(IDX_HEADS * IDX_DH, lambda j, i: (layer, OFF_IQ, 0)),
               (2 * ATT_DH, lambda j, i: (layer, OFF_AK, 0)),
               (LANES, lambda j, i: (layer, OFF_IK, 0))]
    return _project(h, w_t, windows, 1, "project_small")


def _retention_tables(S, R):
    pos = jnp.arange(S, dtype=F32)
    inv = 1.0 / (ROT_BASE ** jnp.linspace(0.0, 1.0, RET_DK // 2, dtype=F32))
    ang = pos[:, None] * inv[None, :]
    log_g = jnp.log(1.0 - 2.0 ** (-5.0 - jnp.arange(RET_HEADS, dtype=F32)))[:, None, None]
    idx = jnp.arange(R, dtype=F32)
    chunk = jnp.arange(R) // CHUNK
    visible = chunk[None, :] <= chunk[:, None]
    k_scale = RET_DK ** -0.5
    assert k_scale == 2.0 ** round(math.log2(k_scale))
    dm = jnp.where(visible[None], jnp.exp(jnp.abs(idx[:, None] - idx[None, :])[None] * log_g), 0.0) * k_scale
    ones = jnp.ones((1, 1, RET_DV), F32)
    qd = jnp.exp((idx[None, :, None] + 1.0) * log_g) * ones
    kd = jnp.exp((R - 1.0 - idx)[None, :, None] * log_g) * ones * k_scale
    cd = jnp.exp(float(R) * log_g) * ones
    return jnp.cos(ang), jnp.sin(ang), dm, qd, kd, cd


def _retention_kernel(q_ref, k_ref, v_ref, g_ref, cos_ref, sin_ref, dm_ref, qd_ref, kd_ref, cd_ref,
                      og_ref, o_ref, state_ref):
    @pl.when(pl.program_id(1) == 0)
    def _():
        state_ref[...] = jnp.zeros_like(state_ref)

    cos = cos_ref[...]
    sin = sin_ref[...]
    half = RET_DK // 2

    def rotated(ref, h):
        x1 = ref[:, h * RET_DK:h * RET_DK + half].astype(F32)
        x2 = ref[:, h * RET_DK + half:(h + 1) * RET_DK].astype(F32)
        return jnp.concatenate([x1 * cos - x2 * sin, x2 * cos + x1 * sin], axis=1)

    for h in range(RET_HEADS):
        cols = slice(h * RET_DV, (h + 1) * RET_DV)
        qr = rotated(q_ref, h)
        kr = rotated(k_ref, h)
        qb = qr.astype(BF16)
        kb = kr.astype(BF16)
        v = v_ref[:, cols]
        s = lax.dot_general(qb, kb, (((1,), (1,)), ((), ())), preferred_element_type=F32) * dm_ref[h]
        o = jnp.dot(s.astype(BF16), v, preferred_element_type=F32)
        state = state_ref[h]
        o = o + jnp.dot(qb, state.astype(BF16), preferred_element_type=F32) * qd_ref[h]
        kdec = (kr * kd_ref[h]).astype(BF16)
        state_ref[h] = state * cd_ref[h] + lax.dot_general(
            kdec, v, (((0,), (0,)), ((), ())), preferred_element_type=F32)
        r = lax.rsqrt(jnp.mean(o * o, axis=-1, keepdims=True) + EPS)
        g = g_ref[:, cols].astype(F32)
        o_ref[:, cols] = (o * r * og_ref[h] * _silu(g)).astype(o_ref.dtype)


def _retention(p_main, tables, ret_out_g, layer, B, S):
    T = p_main.shape[0]
    W = RET_HEADS * RET_DV
    R = tables[2].shape[1]
    nr = S // R
    cos, sin, dm, qd, kd, cd = tables
    row = lambda b, i: b * nr + i
    full3 = lambda a: pl.BlockSpec(a.shape, lambda b, i: (0, 0, 0))
    return pl.pallas_call(
        _retention_kernel,
        grid=(B, nr),
        in_specs=[pl.BlockSpec((R, W), lambda b, i: (row(b, i), 0)),
                  pl.BlockSpec((R, W), lambda b, i: (row(b, i), 1)),
                  pl.BlockSpec((R, W), lambda b, i: (row(b, i), 2)),
                  pl.BlockSpec((R, W), lambda b, i: (row(b, i), 3)),
                  pl.BlockSpec((R, RET_DK // 2), lambda b, i: (i, 0)),
                  pl.BlockSpec((R, RET_DK // 2), lambda b, i: (i, 0)),
                  full3(dm), full3(qd), full3(kd), full3(cd),
                  pl.BlockSpec((None, RET_HEADS, 1, RET_DV), lambda b, i: (layer, 0, 0, 0))],
        out_specs=pl.BlockSpec((R, W), lambda b, i: (row(b, i), 0)),
        out_shape=jax.ShapeDtypeStruct((T, W), BF16),
        scratch_shapes=[pltpu.VMEM((RET_HEADS, RET_DK, RET_DV), F32)],
        compiler_params=_params(2),
        name="retention",
    )(p_main, p_main, p_main, p_main, cos, sin, dm, qd, kd, cd, ret_out_g)


def _prepare_keys_kernel(ak_ref, av_ref, ik_ref, gk_ref, gik_ref, kn_ref, vt_ref, ikn_ref):
    ak = ak_ref[...].astype(F32)
    r = lax.rsqrt(jnp.mean(ak * ak, axis=-1, keepdims=True) + EPS)
    kn_ref[...] = (ak * r * gk_ref[...]).astype(BF16)
    vt_ref[...] = av_ref[...].astype(F32).T.astype(BF16)
    ik = ik_ref[...].astype(F32)
    ik = jnp.where(lax.broadcasted_iota(jnp.int32, ik.shape, 1) < IDX_DH, ik, 0.0)
    r = lax.rsqrt(jnp.sum(ik * ik, axis=-1, keepdims=True) * (1.0 / IDX_DH) + EPS)
    ikn_ref[...] = (ik * r * gik_ref[...]).astype(BF16)


def _prepare_keys(p_small, gk, gik, layer, B, S):
    seg = IDX_HEADS * IDX_DH // LANES
    return pl.pallas_call(
        _prepare_keys_kernel,
        grid=(B,),
        in_specs=[pl.BlockSpec((S, LANES), lambda b: (b, seg)),
                  pl.BlockSpec((S, LANES), lambda b: (b, seg + 1)),
                  pl.BlockSpec((S, LANES), lambda b: (b, seg + 2)),
                  pl.BlockSpec((None, 1, ATT_DH), lambda b: (layer, 0, 0)),
                  pl.BlockSpec((None, 1, LANES), lambda b: (layer, 0, 0))],
        out_specs=[pl.BlockSpec((None, S, ATT_DH), lambda b: (b, 0, 0)),
                   pl.BlockSpec((None, ATT_DH, S), lambda b: (b, 0, 0)),
                   pl.BlockSpec((None, S, LANES), lambda b: (b, 0, 0))],
        out_shape=[jax.ShapeDtypeStruct((B, S, ATT_DH), BF16),
                   jax.ShapeDtypeStruct((B, ATT_DH, S), BF16),
                   jax.ShapeDtypeStruct((B, S, LANES), BF16)],
        compiler_params=_params(1),
        name="prepare_keys",
    )(p_small, p_small, p_small, gk, gik)


def _sparse_attention_kernel(iq_ref, iw_ref, aq_ref, ag_ref, kn_ref, vt_ref, ikn_ref, gq_ref, *rest,
                             topk, first_block):
    o_ref, key_ref, bias_ref, qt_ref, s_ref, p_ref, ot_ref = rest[-7:]
    S = kn_ref.shape[0]
    TQ = Q_BLOCK
    n_sub = iq_ref.shape[0] // TQ
    n_bits = 31
    blocks = [slice(c * KEY_ROWS, (c + 1) * KEY_ROWS) for c in range(S // KEY_ROWS)]

    def scores(sub):
        q_rows = slice(sub * TQ, (sub + 1) * TQ)
        iqT = iq_ref[q_rows, :].astype(F32).T.astype(BF16)
        wT = iw_ref[q_rows, :].astype(F32).T[IDX_DH:IDX_DH + IDX_HEADS, :] * (IDX_HEADS ** -0.5 * IDX_DH ** -0.5)
        no_rows = jnp.zeros((LANES - IDX_DH, 2 * TQ), BF16)
        score = jnp.zeros((S, TQ), F32)
        for hp in range(IDX_HEADS // 2):
            h0, h1 = 2 * hp, 2 * hp + 1
            q_pair = jnp.concatenate([iqT[h0 * IDX_DH:(h0 + 1) * IDX_DH, :],
                                      iqT[h1 * IDX_DH:(h1 + 1) * IDX_DH, :]], axis=1)
            s = jnp.dot(ikn_ref[...], jnp.concatenate([q_pair, no_rows], axis=0), preferred_element_type=F32)
            score = score + jnp.maximum(s[:, :TQ], 0.0) * wT[h0:h0 + 1, :]
            score = score + jnp.maximum(s[:, TQ:], 0.0) * wT[h1:h1 + 1, :]
        bits = lax.bitcast_convert_type(score, jnp.int32)
        key = bits ^ ((bits >> 31) & 0x7FFFFFFF)
        first_query = (first_block + pl.program_id(1) * n_sub + sub) * TQ
        kpos = lax.broadcasted_iota(jnp.int32, (S, TQ), 0)
        qpos = lax.broadcasted_iota(jnp.int32, (S, TQ), 1) + first_query
        key_ref[sub] = jnp.where(kpos // CHUNK <= qpos // CHUNK, key, INT_MIN)

    def count(sub, pred):
        acc = jnp.zeros((KEY_ROWS, TQ), F32)
        for c, rows in enumerate(blocks):
            kpos = lax.broadcasted_iota(jnp.int32, (KEY_ROWS, TQ), 0) + c * KEY_ROWS
            acc = acc + jnp.where(pred(key_ref[sub, rows, :], kpos), 1.0, 0.0)
        return _reduce_rows(acc, jnp.add, jnp.sum)

    def try_threshold(sub, cand, thr, cnt):
        c = count(sub, lambda key, kpos: key >= cand)
        ok = c >= topk
        return jnp.where(ok, cand, thr), jnp.where(ok, c, cnt)

    def sign_step(sub):
        thr = jnp.full((1, TQ), INT_MIN, jnp.int32)
        cnt = jnp.full((1, TQ), float(S), F32)
        return try_threshold(sub, jnp.zeros((1, TQ), jnp.int32), thr, cnt)

    def bit_step(sub, it, thr, cnt):
        return try_threshold(sub, thr | (1 << (n_bits - 1 - it)), thr, cnt)

    def selected_bias(sub, selected):
        for c, rows in enumerate(blocks):
            key = key_ref[sub, rows, :]
            kpos = lax.broadcasted_iota(jnp.int32, (KEY_ROWS, TQ), 0) + c * KEY_ROWS
            finite = (key > KEY_NEG_INF) & (key < KEY_POS_INF)
            b = jnp.where(selected(key, kpos) & finite, 0.0, MASKED_LOGIT)
            bias_ref[sub, rows, :] = b.astype(BF16)

    def select(sub, thr, cnt):
        selected_bias(sub, lambda key, kpos: key >= thr)
        tied = jnp.where((cnt > topk) & (thr > KEY_NEG_INF), 1.0, 0.0)

        @pl.when(jnp.max(tied) > 0.0)
        def _break_ties():
            need = topk - count(sub, lambda key, kpos: key > thr)

            def pos_step(it, bound):
                cand = bound | (1 << (S.bit_length() - 1 - it))
                ok = count(sub, lambda key, kpos: (key == thr) & (kpos < cand)) <= need
                return jnp.where(ok, cand, bound)

            bound = lax.fori_loop(0, S.bit_length(), pos_step, jnp.zeros((1, TQ), jnp.int32))
            selected_bias(sub, lambda key, kpos: (key > thr) | ((key == thr) & (kpos < bound)))

    def normalise_queries(sub):
        aqT = aq_ref[sub * TQ:(sub + 1) * TQ, :].astype(F32).T
        gq = gq_ref[...] * (ATT_DH ** -0.5 * LOG2_E)
        for h in range(ATT_HEADS):
            x = aqT[h * ATT_DH:(h + 1) * ATT_DH, :]
            r = lax.rsqrt(_reduce_rows(x * x, jnp.add, jnp.sum) * (1.0 / ATT_DH) + EPS)
            qt_ref[sub, h * ATT_DH:(h + 1) * ATT_DH, :] = (x * r * gq).astype(BF16)

    n_pairs = ATT_HEADS // 2
    eye = (lax.broadcasted_iota(jnp.int32, (TQ, TQ), 0) == lax.broadcasted_iota(jnp.int32, (TQ, TQ), 1))
    eye = jnp.where(eye, 1.0, 0.0).astype(BF16)
    eye_pair = jnp.concatenate([eye, eye], axis=1)

    def attention(sub):
        def logits_stage(hp, rows, q_pair, top):
            s = jnp.dot(jnp.concatenate([kn_ref[rows, :], bias_ref[sub, rows, :]], axis=1),
                        jnp.concatenate([q_pair, eye_pair], axis=0), preferred_element_type=F32)
            s_ref[hp % 2, rows, :] = s
            return jnp.maximum(top, _fold_rows(s, jnp.maximum))

        def probs_stage(hp, rows, top, den):
            p = jnp.exp2(s_ref[hp % 2, rows, :] - top)
            p_ref[hp % 2, rows, :] = p.astype(BF16)
            return den + _fold_rows(p, jnp.add)

        def values_stage(hp, rows, acc):
            return acc + jnp.dot(vt_ref[:, rows], p_ref[hp % 2, rows, :], preferred_element_type=F32)

        def query_pair(hp):
            row0 = hp * 2 * ATT_DH
            return jnp.concatenate([qt_ref[sub, row0:row0 + ATT_DH, :],
                                    qt_ref[sub, row0 + ATT_DH:row0 + 2 * ATT_DH, :]], axis=1)

        def finish(hp, acc, den):
            oT = acc / den
            row0 = hp * 2 * ATT_DH
            ot_ref[sub, row0:row0 + ATT_DH, :] = oT[:, :TQ]
            ot_ref[sub, row0 + ATT_DH:row0 + 2 * ATT_DH, :] = oT[:, TQ:]

        neg = jnp.full((8, 2 * TQ), -jnp.inf, F32)
        q_next = query_pair(0)
        top_next = neg
        for rows in blocks:
            top_next = logits_stage(0, rows, q_next, top_next)
        den_prev = None
        for hp in range(n_pairs):
            top = jnp.max(top_next, axis=0, keepdims=True)
            if hp + 1 < n_pairs:
                q_next = query_pair(hp + 1)
                top_next = neg
            den = jnp.zeros((8, 2 * TQ), F32)
            acc = jnp.zeros((ATT_DH, 2 * TQ), F32)
            for rows in blocks:
                if hp + 1 < n_pairs:
                    top_next = logits_stage(hp + 1, rows, q_next, top_next)
                den = probs_stage(hp, rows, top, den)
                if hp > 0:
                    acc = values_stage(hp - 1, rows, acc)
            if hp > 0:
                finish(hp - 1, acc, den_prev)
            den_prev = jnp.sum(den, axis=0, keepdims=True)
        acc = jnp.zeros((ATT_DH, 2 * TQ), F32)
        for rows in blocks:
            acc = values_stage(n_pairs - 1, rows, acc)
        finish(n_pairs - 1, acc, den_prev)

    for sub in range(n_sub):
        scores(sub)
    found = tuple(sign_step(sub) for sub in range(n_sub))
    found = lax.fori_loop(
        0, n_bits, lambda it, found: tuple(bit_step(sub, it, *found[sub]) for sub in range(n_sub)), found)
    for sub in range(n_sub):
        select(sub, *found[sub])
        normalise_queries(sub)
    for sub in range(n_sub):
        attention(sub)

    for sub in range(n_sub):
        q_rows = slice(sub * TQ, (sub + 1) * TQ)
        o_ref[q_rows, :] = (ot_ref[sub].T * _silu(ag_ref[q_rows, :].astype(F32))).astype(o_ref.dtype)


def _sparse_attention(p_main, p_small, keys, gq, out, layer, B, S):
    T = p_main.shape[0]
    W = ATT_HEADS * ATT_DH
    TQ = Q_BLOCK
    nb = S // TQ
    topk = min(INDEX_TOPK, S // 4)
    n_iq = IDX_HEADS * IDX_DH
    seg = n_iq // LANES
    kn, vt, ikn = keys
    group_keys = max(KEY_ROWS, S // 8)
    group_blocks = group_keys // TQ
    n_sub = 2
    assert group_blocks % n_sub == 0
    rows = n_sub * TQ
    assert out.shape == (T, W) and out.dtype == BF16
    for first_block in range(0, nb, group_blocks):
        n_keys = (first_block + group_blocks) * TQ
        qrow = lambda b, i, first_block=first_block: (b * nb + first_block) // n_sub + i
        in_specs = [pl.BlockSpec((rows, n_iq), lambda b, i: (qrow(b, i), 0)),
                    pl.BlockSpec((rows, LANES), lambda b, i: (qrow(b, i), seg + 2)),
                    pl.BlockSpec((rows, W), lambda b, i: (qrow(b, i), 4)),
                    pl.BlockSpec((rows, W), lambda b, i: (qrow(b, i), 5)),
                    pl.BlockSpec((None, n_keys, ATT_DH), lambda b, i: (b, 0, 0)),
                    pl.BlockSpec((None, ATT_DH, n_keys), lambda b, i: (b, 0, 0)),
                    pl.BlockSpec((None, n_keys, LANES), lambda b, i: (b, 0, 0)),
                    pl.BlockSpec((None, ATT_DH, 1), lambda b, i: (layer, 0, 0)),
                    pl.BlockSpec(memory_space=pl.ANY)]
        args = [p_small, p_small, p_main, p_main, kn, vt, ikn, gq, out]
        out = pl.pallas_call(
            functools.partial(_sparse_attention_kernel, topk=topk, first_block=first_block),
            grid=(B, group_blocks // n_sub),
            in_specs=in_specs,
            out_specs=pl.BlockSpec((rows, W), lambda b, i: (qrow(b, i), 0)),
            out_shape=jax.ShapeDtypeStruct((T, W), BF16),
            scratch_shapes=[pltpu.VMEM((n_sub, n_keys, TQ), jnp.int32),
                            pltpu.VMEM((n_sub, n_keys, TQ), BF16),
                            pltpu.VMEM((n_sub, W, TQ), BF16),
                            pltpu.VMEM((2, n_keys, 2 * TQ), F32),
                            pltpu.VMEM((2, n_keys, 2 * TQ), BF16),
                            pltpu.VMEM((n_sub, W, TQ), F32)],
            input_output_aliases={len(args) - 1: 0},
            compiler_params=_params(2),
            name="sparse_attention",
        )(*args)
    return out


def _merge_kernel(ar_ref, aa_ref, wr_ref, wa_ref, ga_ref, gb_ref, m_ref, wrb_ref, wab_ref):
    @pl.when(pl.program_id(1) == 0)
    def _():
        wrb_ref[...] = wr_ref[...].astype(BF16)
        wab_ref[...] = wa_ref[...].astype(BF16)

    u_ret = jnp.dot(ar_ref[...], wrb_ref[...], preferred_element_type=F32)
    u_att = jnp.dot(aa_ref[...], wab_ref[...], preferred_element_type=F32)
    m = _sigmoid(ga_ref[...].astype(F32)) * u_ret + _sigmoid(gb_ref[...].astype(F32)) * u_att
    m_ref[...] = m.astype(m_ref.dtype)


def _merge(a_ret, a_att, w_ret, w_att, p_main, layer):
    T, K = a_ret.shape
    D = w_ret.shape[-1]
    tm = _tile(T, 512)
    tn = _tile(D, 1024)
    gate_a = 6 * (D // tn)
    gate_b = 7 * (D // tn)
    w_spec = pl.BlockSpec((None, K, tn), lambda j, i: (layer, 0, j), pipeline_mode=pl.Buffered(1))
    return pl.pallas_call(
        _merge_kernel,
        grid=(D // tn, T // tm),
        in_specs=[pl.BlockSpec((tm, K), lambda j, i: (i, 0)),
                  pl.BlockSpec((tm, K), lambda j, i: (i, 0)),
                  w_spec, w_spec,
                  pl.BlockSpec((tm, tn), lambda j, i: (i, gate_a + j)),
                  pl.BlockSpec((tm, tn), lambda j, i: (i, gate_b + j))],
        out_specs=pl.BlockSpec((tm, tn), lambda j, i: (i, j)),
        out_shape=jax.ShapeDtypeStruct((T, D), BF16),
        scratch_shapes=[pltpu.VMEM((K, tn), BF16), pltpu.VMEM((K, tn), BF16)],
        compiler_params=_params(2),
        name="merge",
    )(a_ret, a_att, w_ret, w_att, p_main, p_main)


def _output_kernel(m_ref, w_ref, x_ref, *refs):
    wb_ref = refs[-1]

    @pl.when(pl.program_id(0) == 0)
    def _():
        wb_ref[...] = w_ref[...].astype(BF16)

    x = x_ref[...] + jnp.dot(m_ref[...], wb_ref[...], preferred_element_type=F32)
    if len(refs) == 2:
        refs[0][...] = x
    else:
        g_ref, xo_ref, h_ref = refs[:3]
        xo_ref[...] = x
        r = lax.rsqrt(jnp.mean(x * x, axis=-1, keepdims=True) + EPS)
        h_ref[...] = (x * r * g_ref[...]).astype(h_ref.dtype)


def _output(m, w_out, x, norm_g, layer, last):
    T, D = x.shape
    tm = _tile(T, 256)
    row_spec = pl.BlockSpec((tm, D), lambda i: (i, 0))
    w_spec = pl.BlockSpec((None, D, D), lambda i: (layer, 0, 0), pipeline_mode=pl.Buffered(1))
    in_specs = [row_spec, w_spec, row_spec]
    scratch = [pltpu.VMEM((D, D), BF16)]
    if last:
        return pl.pallas_call(
            _output_kernel, grid=(T // tm,), in_specs=in_specs, out_specs=row_spec,
            out_shape=jax.ShapeDtypeStruct((T, D), F32), scratch_shapes=scratch,
            compiler_params=_params(1), name="output_last",
        )(m, w_out, x), None
    return pl.pallas_call(
        _output_kernel, grid=(T // tm,),
        in_specs=in_specs + [pl.BlockSpec((None, 1, D), lambda i: (layer + 1, 0, 0))],
        out_specs=[row_spec, row_spec],
        out_shape=[jax.ShapeDtypeStruct((T, D), F32), jax.ShapeDtypeStruct((T, D), BF16)],
        scratch_shapes=scratch, compiler_params=_params(1), name="output",
    )(m, w_out, x, norm_g)


def kernel(x, norm_g, w_in, ret_out_g, att_q_g, att_k_g, idx_k_g, w_branch_ret, w_branch_att, w_out):
    B, S, D = x.shape
    depth = w_in.shape[0]
    T = B * S
    assert w_in.shape[-1] == N_IN and S % KEY_ROWS == 0
    norm_g3 = norm_g.reshape(depth, 1, D)
    ret_g4 = ret_out_g.reshape(depth, RET_HEADS, 1, RET_DV)
    gq = att_q_g.reshape(depth, ATT_DH, 1)
    gk = att_k_g.reshape(depth, 1, ATT_DH)
    gik = jnp.pad(idx_k_g, ((0, 0), (0, LANES - IDX_DH))).reshape(depth, 1, LANES)
    tables = _retention_tables(S, _tile(S, 256))
    w_t = jnp.swapaxes(w_in, 1, 2)

    xf = x.reshape(T, D)
    h = _rmsnorm(xf, norm_g3, 0)
    for layer in range(depth):
        p_main = _project_main(h, w_t, layer)
        p_small = _project_small(h, w_t, layer)
        a_ret = _retention(p_main, tables, ret_g4, layer, B, S)
        keys = _prepare_keys(p_small, gk, gik, layer, B, S)
        a_att = _sparse_attention(p_main, p_small, keys, gq, h, layer, B, S)
        m = _merge(a_ret, a_att, w_branch_ret, w_branch_att, p_main, layer)
        xf, h = _output(m, w_out, xf, norm_g3, layer, last=layer == depth - 1)
    return xf.reshape(B, S, D)
```

```python
import functools
import math

import jax
import jax.numpy as jnp
from jax import lax
from jax.experimental import pallas as pl
from jax.experimental.pallas import tpu as pltpu

F32 = jnp.float32
BF16 = jnp.bfloat16

CHUNK = 64
EPS = 1e-6
RET_HEADS = 8
RET_DK = 256
RET_DV = 256
ROT_BASE = 10000.0
ATT_HEADS = 16
ATT_DH = 128
IDX_HEADS = 16
IDX_DH = 64
INDEX_TOPK = 256
Q_BLOCK = 128
LANES = 128
KEY_ROWS = 256

INT_MIN = -(2 ** 31)
KEY_NEG_INF = -2139095041
KEY_POS_INF = 2139095040
MASKED_LOGIT = -1e30
LOG2_E = 1.4426950408889634
MAX_LOGIT_CAP = 60.0
LOGIT_CAP_MARGIN = 1.01

VMEM_LIMIT_BYTES = 56 * 1024 * 1024


def _params(n_axes):
    return pltpu.CompilerParams(dimension_semantics=("arbitrary",) * n_axes,
                                vmem_limit_bytes=VMEM_LIMIT_BYTES)


def _tile(n, pref):
    t = min(n, pref)
    assert n % t == 0, (n, t)
    return t


def _sigmoid(g):
    return 0.5 * jnp.tanh(0.5 * g) + 0.5


def _silu(g):
    half = 0.5 * g
    return half * jnp.tanh(half) + half


def _fold_rows(x, op):
    SUBLANES = 8
    while x.shape[0] > SUBLANES and x.shape[0] % (2 * SUBLANES) == 0:
        half = x.shape[0] // 2
        x = op(x[:half], x[half:])
    return x


def _reduce_rows(x, op, finish):
    return finish(_fold_rows(x, op), axis=0, keepdims=True)


def _rmsnorm_kernel(x_ref, g_ref, h_ref):
    x = x_ref[...]
    r = lax.rsqrt(jnp.mean(x * x, axis=-1, keepdims=True) + EPS)
    h_ref[...] = (x * r * g_ref[...]).astype(h_ref.dtype)


def _rmsnorm(x, norm_g, layer):
    T, D = x.shape
    tm = _tile(T, 512)
    return pl.pallas_call(
        _rmsnorm_kernel,
        grid=(T // tm,),
        in_specs=[pl.BlockSpec((tm, D), lambda i: (i, 0)),
                  pl.BlockSpec((None, 1, D), lambda i: (layer, 0, 0))],
        out_specs=pl.BlockSpec((tm, D), lambda i: (i, 0)),
        out_shape=jax.ShapeDtypeStruct((T, D), BF16),
        compiler_params=_params(1),
        name="rmsnorm",
    )(x, norm_g)


OFF_AQ = 4 * RET_HEADS * RET_DK
OFF_AK = OFF_AQ + ATT_HEADS * ATT_DH
OFF_AV = OFF_AK + ATT_DH
OFF_AG = OFF_AV + ATT_DH
OFF_IQ = OFF_AG + ATT_HEADS * ATT_DH
OFF_IK = OFF_IQ + IDX_HEADS * IDX_DH
OFF_IW = OFF_IK + IDX_DH
OFF_GA = OFF_IW + IDX_HEADS
N_IN = OFF_GA + 2 * RET_HEADS * RET_DK


def _project_kernel(a_ref, *refs):
    *w_refs, o_ref, wb_ref = refs
    K = wb_ref.shape[0]
    STRIP = 512

    @pl.when(pl.program_id(1) == 0)
    def _():
        for k0 in range(0, K, STRIP):
            parts = [r[0, :, k0:k0 + STRIP] for r in w_refs]
            w = parts[0] if len(parts) == 1 else jnp.concatenate(parts, axis=0)
            wb_ref[k0:k0 + STRIP, :] = w.T.astype(BF16)

    o_ref[...] = jnp.dot(a_ref[...], wb_ref[...], preferred_element_type=F32).astype(o_ref.dtype)


def _project(h, w_t, windows, n_tiles, name):
    T, K = h.shape
    tm = _tile(T, 1024)
    tn = sum(rows for rows, _ in windows)
    specs = [pl.BlockSpec((pl.Element(1), pl.Element(rows), pl.Element(K)), index_map)
             for rows, index_map in windows]
    return pl.pallas_call(
        _project_kernel,
        grid=(n_tiles, T // tm),
        in_specs=[pl.BlockSpec((tm, K), lambda j, i: (i, 0))] + specs,
        out_specs=pl.BlockSpec((tm, tn), lambda j, i: (i, j)),
        out_shape=jax.ShapeDtypeStruct((T, n_tiles * tn), BF16),
        scratch_shapes=[pltpu.VMEM((K, tn), BF16)],
        compiler_params=_params(2),
        name=name,
    )(h, *([w_t] * len(specs)))


def _project_main(h, w_t, layer):
    tn = 1024
    SUB = 8
    runs = ((0, OFF_AK), (OFF_AG, OFF_IQ), (OFF_GA, N_IN))
    assert all(a % SUB == 0 and (b - a) % tn == 0 for a, b in runs)
    tiles_before = [sum((b - a) // tn for a, b in runs[:r]) for r in range(len(runs) + 1)]

    def first_row(j):
        row = j * (tn // SUB)
        for r in range(1, len(runs)):
            gap = runs[r][0] - runs[r - 1][1]
            row = row + jnp.where(j >= tiles_before[r], gap // SUB, 0)
        return row * SUB

    return _project(h, w_t, [(tn, lambda j, i: (layer, first_row(j), 0))], tiles_before[-1], "project_main")


def _project_small(h, w_t, layer):
    windows = [(IDX_HEADS * IDX_DH, lambda j, i: (layer, OFF_IQ, 0)),
               (2 * ATT_DH, lambda j, i: (layer, OFF_AK, 0)),
               (LANES, lambda j, i: (layer, OFF_IK, 0))]
    return _project(h, w_t, windows, 1, "project_small")


def _retention_tables(S, R):
    pos = jnp.arange(S, dtype=F32)
    inv = 1.0 / (ROT_BASE ** jnp.linspace(0.0, 1.0, RET_DK // 2, dtype=F32))
    ang = pos[:, None] * inv[None, :]
    log_g = jnp.log(1.0 - 2.0 ** (-5.0 - jnp.arange(RET_HEADS, dtype=F32)))[:, None, None]
    idx = jnp.arange(R, dtype=F32)
    chunk = jnp.arange(R) // CHUNK
    visible = chunk[None, :] <= chunk[:, None]
    k_scale = RET_DK ** -0.5
    assert k_scale == 2.0 ** round(math.log2(k_scale))
    dm = jnp.where(visible[None], jnp.exp(jnp.abs(idx[:, None] - idx[None, :])[None] * log_g), 0.0) * k_scale
    ones = jnp.ones((1, 1, RET_DV), F32)
    qd = jnp.exp((idx[None, :, None] + 1.0) * log_g) * ones
    kd = jnp.exp((R - 1.0 - idx)[None, :, None] * log_g) * ones * k_scale
    cd = jnp.exp(float(R) * log_g) * ones
    return jnp.cos(ang), jnp.sin(ang), dm, qd, kd, cd


def _retention_kernel(q_ref, k_ref, v_ref, g_ref, cos_ref, sin_ref, dm_ref, qd_ref, kd_ref, cd_ref,
                      og_ref, o_ref, state_ref):
    @pl.when(pl.program_id(1) == 0)
    def _():
        state_ref[...] = jnp.zeros_like(state_ref)

    cos = cos_ref[...]
    sin = sin_ref[...]
    half = RET_DK // 2

    def rotated(ref, h):
        x1 = ref[:, h * RET_DK:h * RET_DK + half].astype(F32)
        x2 = ref[:, h * RET_DK + half:(h + 1) * RET_DK].astype(F32)
        return jnp.concatenate([x1 * cos - x2 * sin, x2 * cos + x1 * sin], axis=1)

    for h in range(RET_HEADS):
        cols = slice(h * RET_DV, (h + 1) * RET_DV)
        qr = rotated(q_ref, h)
        kr = rotated(k_ref, h)
        qb = qr.astype(BF16)
        kb = kr.astype(BF16)
        v = v_ref[:, cols]
        s = lax.dot_general(qb, kb, (((1,), (1,)), ((), ())), preferred_element_type=F32) * dm_ref[h]
        o = jnp.dot(s.astype(BF16), v, preferred_element_type=F32)
        state = state_ref[h]
        o = o + jnp.dot(qb, state.astype(BF16), preferred_element_type=F32) * qd_ref[h]
        kdec = (kr * kd_ref[h]).astype(BF16)
        state_ref[h] = state * cd_ref[h] + lax.dot_general(
            kdec, v, (((0,), (0,)), ((), ())), preferred_element_type=F32)
        r = lax.rsqrt(jnp.mean(o * o, axis=-1, keepdims=True) + EPS)
        g = g_ref[:, cols].astype(F32)
        o_ref[:, cols] = (o * r * og_ref[h] * _silu(g)).astype(o_ref.dtype)


def _retention(p_main, tables, ret_out_g, layer, B, S):
    T = p_main.shape[0]
    W = RET_HEADS * RET_DV
    R = tables[2].shape[1]
    nr = S // R
    cos, sin, dm, qd, kd, cd = tables
    row = lambda b, i: b * nr + i
    full3 = lambda a: pl.BlockSpec(a.shape, lambda b, i: (0, 0, 0))
    return pl.pallas_call(
        _retention_kernel,
        grid=(B, nr),
        in_specs=[pl.BlockSpec((R, W), lambda b, i: (row(b, i), 0)),
                  pl.BlockSpec((R, W), lambda b, i: (row(b, i), 1)),
                  pl.BlockSpec((R, W), lambda b, i: (row(b, i), 2)),
                  pl.BlockSpec((R, W), lambda b, i: (row(b, i), 3)),
                  pl.BlockSpec((R, RET_DK // 2), lambda b, i: (i, 0)),
                  pl.BlockSpec((R, RET_DK // 2), lambda b, i: (i, 0)),
                  full3(dm), full3(qd), full3(kd), full3(cd),
                  pl.BlockSpec((None, RET_HEADS, 1, RET_DV), lambda b, i: (layer, 0, 0, 0))],
        out_specs=pl.BlockSpec((R, W), lambda b, i: (row(b, i), 0)),
        out_shape=jax.ShapeDtypeStruct((T, W), BF16),
        scratch_shapes=[pltpu.VMEM((RET_HEADS, RET_DK, RET_DV), F32)],
        compiler_params=_params(2),
        name="retention",
    )(p_main, p_main, p_main, p_main, cos, sin, dm, qd, kd, cd, ret_out_g)


def _prepare_keys_kernel(ak_ref, av_ref, ik_ref, gk_ref, gik_ref, kn_ref, vt_ref, ikn_ref, kmax_ref):
    ak = ak_ref[...].astype(F32)
    r = lax.rsqrt(jnp.mean(ak * ak, axis=-1, keepdims=True) + EPS)
    kn = (ak * r * gk_ref[...]).astype(BF16)
    kn_ref[...] = kn
    kn = kn.astype(F32)
    kmax_ref[...] = jnp.broadcast_to(jnp.sqrt(jnp.max(jnp.sum(kn * kn, axis=-1, keepdims=True))), kmax_ref.shape)
    vt_ref[...] = av_ref[...].astype(F32).T.astype(BF16)
    ik = ik_ref[...].astype(F32)
    ik = jnp.where(lax.broadcasted_iota(jnp.int32, ik.shape, 1) < IDX_DH, ik, 0.0)
    r = lax.rsqrt(jnp.sum(ik * ik, axis=-1, keepdims=True) * (1.0 / IDX_DH) + EPS)
    ikn_ref[...] = (ik * r * gik_ref[...]).astype(BF16)


def _prepare_keys(p_small, gk, gik, layer, B, S):
    seg = IDX_HEADS * IDX_DH // LANES
    return pl.pallas_call(
        _prepare_keys_kernel,
        grid=(B,),
        in_specs=[pl.BlockSpec((S, LANES), lambda b: (b, seg)),
                  pl.BlockSpec((S, LANES), lambda b: (b, seg + 1)),
                  pl.BlockSpec((S, LANES), lambda b: (b, seg + 2)),
                  pl.BlockSpec((None, 1, ATT_DH), lambda b: (layer, 0, 0)),
                  pl.BlockSpec((None, 1, LANES), lambda b: (layer, 0, 0))],
        out_specs=[pl.BlockSpec((None, S, ATT_DH), lambda b: (b, 0, 0)),
                   pl.BlockSpec((None, ATT_DH, S), lambda b: (b, 0, 0)),
                   pl.BlockSpec((None, S, LANES), lambda b: (b, 0, 0)),
                   pl.BlockSpec((None, 8, LANES), lambda b: (b, 0, 0))],
        out_shape=[jax.ShapeDtypeStruct((B, S, ATT_DH), BF16),
                   jax.ShapeDtypeStruct((B, ATT_DH, S), BF16),
                   jax.ShapeDtypeStruct((B, S, LANES), BF16),
                   jax.ShapeDtypeStruct((B, 8, LANES), F32)],
        compiler_params=_params(1),
        name="prepare_keys",
    )(p_small, p_small, p_small, gk, gik)


def _sparse_attention_kernel(iq_ref, iw_ref, aq_ref, ag_ref, kn_ref, vt_ref, ikn_ref, kmax_ref, gq_ref, *rest,
                             topk, first_block):
    o_ref, key_ref, bias_ref, qt_ref, s_ref, p_ref, ot_ref, cap_ref = rest[-8:]
    S = kn_ref.shape[0]
    TQ = Q_BLOCK
    n_sub = iq_ref.shape[0] // TQ
    n_bits = 31
    blocks = [slice(c * KEY_ROWS, (c + 1) * KEY_ROWS) for c in range(S // KEY_ROWS)]

    def scores(sub):
        q_rows = slice(sub * TQ, (sub + 1) * TQ)
        iqT = iq_ref[q_rows, :].astype(F32).T.astype(BF16)
        wT = iw_ref[q_rows, :].astype(F32).T[IDX_DH:IDX_DH + IDX_HEADS, :] * (IDX_HEADS ** -0.5 * IDX_DH ** -0.5)
        no_rows = jnp.zeros((LANES - IDX_DH, 2 * TQ), BF16)
        score = jnp.zeros((S, TQ), F32)
        for hp in range(IDX_HEADS // 2):
            h0, h1 = 2 * hp, 2 * hp + 1
            q_pair = jnp.concatenate([iqT[h0 * IDX_DH:(h0 + 1) * IDX_DH, :],
                                      iqT[h1 * IDX_DH:(h1 + 1) * IDX_DH, :]], axis=1)
            s = jnp.dot(ikn_ref[...], jnp.concatenate([q_pair, no_rows], axis=0), preferred_element_type=F32)
            score = score + jnp.maximum(s[:, :TQ], 0.0) * wT[h0:h0 + 1, :]
            score = score + jnp.maximum(s[:, TQ:], 0.0) * wT[h1:h1 + 1, :]
        bits = lax.bitcast_convert_type(score, jnp.int32)
        key = bits ^ ((bits >> 31) & 0x7FFFFFFF)
        first_query = (first_block + pl.program_id(1) * n_sub + sub) * TQ
        kpos = lax.broadcasted_iota(jnp.int32, (S, TQ), 0)
        qpos = lax.broadcasted_iota(jnp.int32, (S, TQ), 1) + first_query
        key_ref[sub] = jnp.where(kpos // CHUNK <= qpos // CHUNK, key, INT_MIN)

    def count(sub, pred):
        acc = jnp.zeros((KEY_ROWS, TQ), F32)
        for c, rows in enumerate(blocks):
            kpos = lax.broadcasted_iota(jnp.int32, (KEY_ROWS, TQ), 0) + c * KEY_ROWS
            acc = acc + jnp.where(pred(key_ref[sub, rows, :], kpos), 1.0, 0.0)
        return _reduce_rows(acc, jnp.add, jnp.sum)

    def try_threshold(sub, cand, thr, cnt):
        c = count(sub, lambda key, kpos: key >= cand)
        ok = c >= topk
        return jnp.where(ok, cand, thr), jnp.where(ok, c, cnt)

    def sign_step(sub):
        thr = jnp.full((1, TQ), INT_MIN, jnp.int32)
        cnt = jnp.full((1, TQ), float(S), F32)
        return try_threshold(sub, jnp.zeros((1, TQ), jnp.int32), thr, cnt)

    def bit_step(sub, it, thr, cnt):
        return try_threshold(sub, thr | (1 << (n_bits - 1 - it)), thr, cnt)

    def selected_bias(sub, selected):
        for c, rows in enumerate(blocks):
            key = key_ref[sub, rows, :]
            kpos = lax.broadcasted_iota(jnp.int32, (KEY_ROWS, TQ), 0) + c * KEY_ROWS
            finite = (key > KEY_NEG_INF) & (key < KEY_POS_INF)
            b = jnp.where(selected(key, kpos) & finite, 0.0, MASKED_LOGIT)
            bias_ref[sub, rows, :] = b.astype(BF16)

    def select(sub, thr, cnt):
        selected_bias(sub, lambda key, kpos: key >= thr)
        tied = jnp.where((cnt > topk) & (thr > KEY_NEG_INF), 1.0, 0.0)

        @pl.when(jnp.max(tied) > 0.0)
        def _break_ties():
            need = topk - count(sub, lambda key, kpos: key > thr)

            def pos_step(it, bound):
                cand = bound | (1 << (S.bit_length() - 1 - it))
                ok = count(sub, lambda key, kpos: (key == thr) & (kpos < cand)) <= need
                return jnp.where(ok, cand, bound)

            bound = lax.fori_loop(0, S.bit_length(), pos_step, jnp.zeros((1, TQ), jnp.int32))
            selected_bias(sub, lambda key, kpos: (key > thr) | ((key == thr) & (kpos < bound)))

    def normalise_queries(sub):
        aqT = aq_ref[sub * TQ:(sub + 1) * TQ, :].astype(F32).T
        gq = gq_ref[...] * (ATT_DH ** -0.5 * LOG2_E)
        kmax = kmax_ref[0:1, :] * LOGIT_CAP_MARGIN
        for h in range(ATT_HEADS):
            x = aqT[h * ATT_DH:(h + 1) * ATT_DH, :]
            r = lax.rsqrt(_reduce_rows(x * x, jnp.add, jnp.sum) * (1.0 / ATT_DH) + EPS)
            qn = x * r * gq
            qt_ref[sub, h * ATT_DH:(h + 1) * ATT_DH, :] = qn.astype(BF16)
            cap_ref[sub, h:h + 1, :] = jnp.sqrt(_reduce_rows(qn * qn, jnp.add, jnp.sum)) * kmax

    n_pairs = ATT_HEADS // 2
    eye = (lax.broadcasted_iota(jnp.int32, (TQ, TQ), 0) == lax.broadcasted_iota(jnp.int32, (TQ, TQ), 1))
    eye = jnp.where(eye, 1.0, 0.0).astype(BF16)
    eye_pair = jnp.concatenate([eye, eye], axis=1)

    def attention(sub, capped):
        def logits_stage(hp, rows, q_pair, top):
            s = jnp.dot(jnp.concatenate([kn_ref[rows, :], bias_ref[sub, rows, :]], axis=1),
                        jnp.concatenate([q_pair, eye_pair], axis=0), preferred_element_type=F32)
            s_ref[hp % 2, rows, :] = s
            return top if capped else jnp.maximum(top, _fold_rows(s, jnp.maximum))

        def shift(hp, top):
            if capped:
                return jnp.concatenate([cap_ref[sub, 2 * hp:2 * hp + 1, :], cap_ref[sub, 2 * hp + 1:2 * hp + 2, :]],
                                       axis=1)
            return jnp.max(top, axis=0, keepdims=True)

        def probs_stage(hp, rows, top, den):
            p = jnp.exp2(s_ref[hp % 2, rows, :] - top)
            p_ref[hp % 2, rows, :] = p.astype(BF16)
            return den + _fold_rows(p, jnp.add)

        def values_stage(hp, rows, acc):
            return acc + jnp.dot(vt_ref[:, rows], p_ref[hp % 2, rows, :], preferred_element_type=F32)

        def query_pair(hp):
            row0 = hp * 2 * ATT_DH
            return jnp.concatenate([qt_ref[sub, row0:row0 + ATT_DH, :],
                                    qt_ref[sub, row0 + ATT_DH:row0 + 2 * ATT_DH, :]], axis=1)

        def finish(hp, acc, den):
            oT = acc / den
            row0 = hp * 2 * ATT_DH
            ot_ref[sub, row0:row0 + ATT_DH, :] = oT[:, :TQ]
            ot_ref[sub, row0 + ATT_DH:row0 + 2 * ATT_DH, :] = oT[:, TQ:]

        neg = jnp.full((8, 2 * TQ), -jnp.inf, F32)
        q_next = query_pair(0)
        top_next = neg
        for rows in blocks:
            top_next = logits_stage(0, rows, q_next, top_next)
        den_prev = None
        for hp in range(n_pairs):
            top = shift(hp, top_next)
            if hp + 1 < n_pairs:
                q_next = query_pair(hp + 1)
                top_next = neg
            den = jnp.zeros((8, 2 * TQ), F32)
            acc = jnp.zeros((ATT_DH, 2 * TQ), F32)
            for rows in blocks:
                if hp + 1 < n_pairs:
                    top_next = logits_stage(hp + 1, rows, q_next, top_next)
                den = probs_stage(hp, rows, top, den)
                if hp > 0:
                    acc = values_stage(hp - 1, rows, acc)
            if hp > 0:
                finish(hp - 1, acc, den_prev)
            den_prev = jnp.sum(den, axis=0, keepdims=True)
        acc = jnp.zeros((ATT_DH, 2 * TQ), F32)
        for rows in blocks:
            acc = values_stage(n_pairs - 1, rows, acc)
        finish(n_pairs - 1, acc, den_prev)

    for sub in range(n_sub):
        scores(sub)
    found = tuple(sign_step(sub) for sub in range(n_sub))
    found = lax.fori_loop(
        0, n_bits, lambda it, found: tuple(bit_step(sub, it, *found[sub]) for sub in range(n_sub)), found)
    for sub in range(n_sub):
        select(sub, *found[sub])
        normalise_queries(sub)
    for sub in range(n_sub):
        capped = jnp.max(cap_ref[sub]) <= MAX_LOGIT_CAP
        pl.when(capped)(functools.partial(attention, sub, True))
        pl.when(jnp.logical_not(capped))(functools.partial(attention, sub, False))

    for sub in range(n_sub):
        q_rows = slice(sub * TQ, (sub + 1) * TQ)
        o_ref[q_rows, :] = (ot_ref[sub].T * _silu(ag_ref[q_rows, :].astype(F32))).astype(o_ref.dtype)


def _sparse_attention(p_main, p_small, keys, gq, out, layer, B, S):
    T = p_main.shape[0]
    W = ATT_HEADS * ATT_DH
    TQ = Q_BLOCK
    nb = S // TQ
    topk = min(INDEX_TOPK, S // 4)
    n_iq = IDX_HEADS * IDX_DH
    seg = n_iq // LANES
    kn, vt, ikn, kmax = keys
    group_keys = max(KEY_ROWS, S // 8)
    group_blocks = group_keys // TQ
    n_sub = 2
    assert group_blocks % n_sub == 0
    rows = n_sub * TQ
    assert out.shape == (T, W) and out.dtype == BF16
    for first_block in range(0, nb, group_blocks):
        n_keys = (first_block + group_blocks) * TQ
        qrow = lambda b, i, first_block=first_block: (b * nb + first_block) // n_sub + i
        in_specs = [pl.BlockSpec((rows, n_iq), lambda b, i: (qrow(b, i), 0)),
                    pl.BlockSpec((rows, LANES), lambda b, i: (qrow(b, i), seg + 2)),
                    pl.BlockSpec((rows, W), lambda b, i: (qrow(b, i), 4)),
                    pl.BlockSpec((rows, W), lambda b, i: (qrow(b, i), 5)),
                    pl.BlockSpec((None, n_keys, ATT_DH), lambda b, i: (b, 0, 0)),
                    pl.BlockSpec((None, ATT_DH, n_keys), lambda b, i: (b, 0, 0)),
                    pl.BlockSpec((None, n_keys, LANES), lambda b, i: (b, 0, 0)),
                    pl.BlockSpec((None, 8, LANES), lambda b, i: (b, 0, 0)),
                    pl.BlockSpec((None, ATT_DH, 1), lambda b, i: (layer, 0, 0)),
                    pl.BlockSpec(memory_space=pl.ANY)]
        args = [p_small, p_small, p_main, p_main, kn, vt, ikn, kmax, gq, out]
        out = pl.pallas_call(
            functools.partial(_sparse_attention_kernel, topk=topk, first_block=first_block),
            grid=(B, group_blocks // n_sub),
            in_specs=in_specs,
            out_specs=pl.BlockSpec((rows, W), lambda b, i: (qrow(b, i), 0)),
            out_shape=jax.ShapeDtypeStruct((T, W), BF16),
            scratch_shapes=[pltpu.VMEM((n_sub, n_keys, TQ), jnp.int32),
                            pltpu.VMEM((n_sub, n_keys, TQ), BF16),
                            pltpu.VMEM((n_sub, W, TQ), BF16),
                            pltpu.VMEM((2, n_keys, 2 * TQ), F32),
                            pltpu.VMEM((2, n_keys, 2 * TQ), BF16),
                            pltpu.VMEM((n_sub, W, TQ), F32),
                            pltpu.VMEM((n_sub, ATT_HEADS, TQ), F32)],
            input_output_aliases={len(args) - 1: 0},
            compiler_params=_params(2),
            name="sparse_attention",
        )(*args)
    return out


def _merge_kernel(ar_ref, aa_ref, wr_ref, wa_ref, ga_ref, gb_ref, m_ref, wrb_ref, wab_ref):
    @pl.when(pl.program_id(1) == 0)
    def _():
        wrb_ref[...] = wr_ref[...].astype(BF16)
        wab_ref[...] = wa_ref[...].astype(BF16)

    u_ret = jnp.dot(ar_ref[...], wrb_ref[...], preferred_element_type=F32)
    u_att = jnp.dot(aa_ref[...], wab_ref[...], preferred_element_type=F32)
    m = _sigmoid(ga_ref[...].astype(F32)) * u_ret + _sigmoid(gb_ref[...].astype(F32)) * u_att
    m_ref[...] = m.astype(m_ref.dtype)


def _merge(a_ret, a_att, w_ret, w_att, p_main, layer):
    T, K = a_ret.shape
    D = w_ret.shape[-1]
    tm = _tile(T, 512)
    tn = _tile(D, 1024)
    gate_a = 6 * (D // tn)
    gate_b = 7 * (D // tn)
    w_spec = pl.BlockSpec((None, K, tn), lambda j, i: (layer, 0, j), pipeline_mode=pl.Buffered(1))
    return pl.pallas_call(
        _merge_kernel,
        grid=(D // tn, T // tm),
        in_specs=[pl.BlockSpec((tm, K), lambda j, i: (i, 0)),
                  pl.BlockSpec((tm, K), lambda j, i: (i, 0)),
                  w_spec, w_spec,
                  pl.BlockSpec((tm, tn), lambda j, i: (i, gate_a + j)),
                  pl.BlockSpec((tm, tn), lambda j, i: (i, gate_b + j))],
        out_specs=pl.BlockSpec((tm, tn), lambda j, i: (i, j)),
        out_shape=jax.ShapeDtypeStruct((T, D), BF16),
        scratch_shapes=[pltpu.VMEM((K, tn), BF16), pltpu.VMEM((K, tn), BF16)],
        compiler_params=_params(2),
        name="merge",
    )(a_ret, a_att, w_ret, w_att, p_main, p_main)


def _output_kernel(m_ref, w_ref, x_ref, *refs):
    wb_ref = refs[-1]

    @pl.when(pl.program_id(0) == 0)
    def _():
        wb_ref[...] = w_ref[...].astype(BF16)

    x = x_ref[...] + jnp.dot(m_ref[...], wb_ref[...], preferred_element_type=F32)
    if len(refs) == 2:
        refs[0][...] = x
    else:
        g_ref, xo_ref, h_ref = refs[:3]
        xo_ref[...] = x
        r = lax.rsqrt(jnp.mean(x * x, axis=-1, keepdims=True) + EPS)
        h_ref[...] = (x * r * g_ref[...]).astype(h_ref.dtype)


def _output(m, w_out, x, norm_g, layer, last):
    T, D = x.shape
    tm = _tile(T, 256)
    row_spec = pl.BlockSpec((tm, D), lambda i: (i, 0))
    w_spec = pl.BlockSpec((None, D, D), lambda i: (layer, 0, 0), pipeline_mode=pl.Buffered(1))
    in_specs = [row_spec, w_spec, row_spec]
    scratch = [pltpu.VMEM((D, D), BF16)]
    if last:
        return pl.pallas_call(
            _output_kernel, grid=(T // tm,), in_specs=in_specs, out_specs=row_spec,
            out_shape=jax.ShapeDtypeStruct((T, D), F32), scratch_shapes=scratch,
            compiler_params=_params(1), name="output_last",
        )(m, w_out, x), None
    return pl.pallas_call(
        _output_kernel, grid=(T // tm,),
        in_specs=in_specs + [pl.BlockSpec((None, 1, D), lambda i: (layer + 1, 0, 0))],
        out_specs=[row_spec, row_spec],
        out_shape=[jax.ShapeDtypeStruct((T, D), F32), jax.ShapeDtypeStruct((T, D), BF16)],
        scratch_shapes=scratch, compiler_params=_params(1), name="output",
    )(m, w_out, x, norm_g)


def kernel(x, norm_g, w_in, ret_out_g, att_q_g, att_k_g, idx_k_g, w_branch_ret, w_branch_att, w_out):
    B, S, D = x.shape
    depth = w_in.shape[0]
    T = B * S
    assert w_in.shape[-1] == N_IN and S % KEY_ROWS == 0
    norm_g3 = norm_g.reshape(depth, 1, D)
    ret_g4 = ret_out_g.reshape(depth, RET_HEADS, 1, RET_DV)
    gq = att_q_g.reshape(depth, ATT_DH, 1)
    gk = att_k_g.reshape(depth, 1, ATT_DH)
    gik = jnp.pad(idx_k_g, ((0, 0), (0, LANES - IDX_DH))).reshape(depth, 1, LANES)
    tables = _retention_tables(S, _tile(S, 256))
    w_t = jnp.swapaxes(w_in, 1, 2)

    xf = x.reshape(T, D)
    h = _rmsnorm(xf, norm_g3, 0)
    for layer in range(depth):
        p_main = _project_main(h, w_t, layer)
        p_small = _project_small(h, w_t, layer)
        a_ret = _retention(p_main, tables, ret_g4, layer, B, S)
        keys = _prepare_keys(p_small, gk, gik, layer, B, S)
        a_att = _sparse_attention(p_main, p_small, keys, gq, h, layer, B, S)
        m = _merge(a_ret, a_att, w_branch_ret, w_branch_att, p_main, layer)
        xf, h = _output(m, w_out, xf, norm_g3, layer, last=layer == depth - 1)
    return xf.reshape(B, S, D)
```

```python
import functools
import math

import jax
import jax.numpy as jnp
from jax import lax
from jax.experimental import pallas as pl
from jax.experimental.pallas import tpu as pltpu

F32 = jnp.float32
BF16 = jnp.bfloat16

CHUNK = 64
EPS = 1e-6
RET_HEADS = 8
RET_DK = 256
RET_DV = 256
ROT_BASE = 10000.0
ATT_HEADS = 16
ATT_DH = 128
IDX_HEADS = 16
IDX_DH = 64
INDEX_TOPK = 256
Q_BLOCK = 128
LANES = 128
KEY_ROWS = 256

INT_MIN = -(2 ** 31)
KEY_NEG_INF = -2139095041
KEY_POS_INF = 2139095040
MASKED_LOGIT = -1e30
LOG2_E = 1.4426950408889634

VMEM_LIMIT_BYTES = 56 * 1024 * 1024


def _params(n_axes):
    return pltpu.CompilerParams(dimension_semantics=("arbitrary",) * n_axes,
                                vmem_limit_bytes=VMEM_LIMIT_BYTES)


def _tile(n, pref):
    t = min(n, pref)
    assert n % t == 0, (n, t)
    return t


def _sigmoid(g):
    return 0.5 * jnp.tanh(0.5 * g) + 0.5


def _silu(g):
    half = 0.5 * g
    return half * jnp.tanh(half) + half


def _fold_rows(x, op):
    SUBLANES = 8
    while x.shape[0] > SUBLANES and x.shape[0] % (2 * SUBLANES) == 0:
        half = x.shape[0] // 2
        x = op(x[:half], x[half:])
    return x


def _reduce_rows(x, op, finish):
    return finish(_fold_rows(x, op), axis=0, keepdims=True)


def _rmsnorm_kernel(x_ref, g_ref, h_ref):
    x = x_ref[...]
    r = lax.rsqrt(jnp.mean(x * x, axis=-1, keepdims=True) + EPS)
    h_ref[...] = (x * r * g_ref[...]).astype(h_ref.dtype)


def _rmsnorm(x, norm_g, layer):
    T, D = x.shape
    tm = _tile(T, 512)
    return pl.pallas_call(
        _rmsnorm_kernel,
        grid=(T // tm,),
        in_specs=[pl.BlockSpec((tm, D), lambda i: (i, 0)),
                  pl.BlockSpec((None, 1, D), lambda i: (layer, 0, 0))],
        out_specs=pl.BlockSpec((tm, D), lambda i: (i, 0)),
        out_shape=jax.ShapeDtypeStruct((T, D), BF16),
        compiler_params=_params(1),
        name="rmsnorm",
    )(x, norm_g)


OFF_AQ = 4 * RET_HEADS * RET_DK
OFF_AK = OFF_AQ + ATT_HEADS * ATT_DH
OFF_AV = OFF_AK + ATT_DH
OFF_AG = OFF_AV + ATT_DH
OFF_IQ = OFF_AG + ATT_HEADS * ATT_DH
OFF_IK = OFF_IQ + IDX_HEADS * IDX_DH
OFF_IW = OFF_IK + IDX_DH
OFF_GA = OFF_IW + IDX_HEADS
N_IN = OFF_GA + 2 * RET_HEADS * RET_DK


def _project_kernel(a_ref, *refs):
    *w_refs, o_ref, wb_ref = refs
    K = wb_ref.shape[0]
    STRIP = 512

    @pl.when(pl.program_id(1) == 0)
    def _():
        for k0 in range(0, K, STRIP):
            parts = [r[0, :, k0:k0 + STRIP] for r in w_refs]
            w = parts[0] if len(parts) == 1 else jnp.concatenate(parts, axis=0)
            wb_ref[k0:k0 + STRIP, :] = w.T.astype(BF16)

    o_ref[...] = jnp.dot(a_ref[...], wb_ref[...], preferred_element_type=F32).astype(o_ref.dtype)


def _project(h, w_t, windows, n_tiles, name):
    T, K = h.shape
    tm = _tile(T, 1024)
    tn = sum(rows for rows, _ in windows)
    specs = [pl.BlockSpec((pl.Element(1), pl.Element(rows), pl.Element(K)), index_map)
             for rows, index_map in windows]
    return pl.pallas_call(
        _project_kernel,
        grid=(n_tiles, T // tm),
        in_specs=[pl.BlockSpec((tm, K), lambda j, i: (i, 0))] + specs,
        out_specs=pl.BlockSpec((tm, tn), lambda j, i: (i, j)),
        out_shape=jax.ShapeDtypeStruct((T, n_tiles * tn), BF16),
        scratch_shapes=[pltpu.VMEM((K, tn), BF16)],
        compiler_params=_params(2),
        name=name,
    )(h, *([w_t] * len(specs)))


def _project_main(h, w_t, layer):
    tn = 1024
    SUB = 8
    runs = ((0, OFF_AK), (OFF_AG, OFF_IQ), (OFF_GA, N_IN))
    assert all(a % SUB == 0 and (b - a) % tn == 0 for a, b in runs)
    tiles_before = [sum((b - a) // tn for a, b in runs[:r]) for r in range(len(runs) + 1)]

    def first_row(j):
        row = j * (tn // SUB)
        for r in range(1, len(runs)):
            gap = runs[r][0] - runs[r - 1][1]
            row = row + jnp.where(j >= tiles_before[r], gap // SUB, 0)
        return row * SUB

    return _project(h, w_t, [(tn, lambda j, i: (layer, first_row(j), 0))], tiles_before[-1], "project_main")


def _project_small(h, w_t, layer):
    windows = [(IDX_HEADS * IDX_DH, lambda j, i: (layer, OFF_IQ, 0)),
               (2 * ATT_DH, lambda j, i: (layer, OFF_AK, 0)),
               (LANES, lambda j, i: (layer, OFF_IK, 0))]
    return _project(h, w_t, windows, 1, "project_small")


def _retention_tables(S, R):
    pos = jnp.arange(S, dtype=F32)
    inv = 1.0 / (ROT_BASE ** jnp.linspace(0.0, 1.0, RET_DK // 2, dtype=F32))
    ang = pos[:, None] * inv[None, :]
    log_g = jnp.log(1.0 - 2.0 ** (-5.0 - jnp.arange(RET_HEADS, dtype=F32)))[:, None, None]
    idx = jnp.arange(R, dtype=F32)
    chunk = jnp.arange(R) // CHUNK
    visible = chunk[None, :] <= chunk[:, None]
    k_scale = RET_DK ** -0.5
    assert k_scale == 2.0 ** round(math.log2(k_scale))
    dm = jnp.where(visible[None], jnp.exp(jnp.abs(idx[:, None] - idx[None, :])[None] * log_g), 0.0) * k_scale
    ones = jnp.ones((1, 1, RET_DV), F32)
    qd = jnp.exp((idx[None, :, None] + 1.0) * log_g) * ones
    kd = jnp.exp((R - 1.0 - idx)[None, :, None] * log_g) * ones * k_scale
    cd = jnp.exp(float(R) * log_g) * ones
    return jnp.cos(ang), jnp.sin(ang), dm, qd, kd, cd


def _retention_kernel(q_ref, k_ref, v_ref, g_ref, cos_ref, sin_ref, dm_ref, qd_ref, kd_ref, cd_ref,
                      og_ref, o_ref, state_ref):
    @pl.when(pl.program_id(1) == 0)
    def _():
        state_ref[...] = jnp.zeros_like(state_ref)

    cos = cos_ref[...]
    sin = sin_ref[...]
    half = RET_DK // 2

    def rotated(ref, h):
        x1 = ref[:, h * RET_DK:h * RET_DK + half].astype(F32)
        x2 = ref[:, h * RET_DK + half:(h + 1) * RET_DK].astype(F32)
        return jnp.concatenate([x1 * cos - x2 * sin, x2 * cos + x1 * sin], axis=1)

    for h in range(RET_HEADS):
        cols = slice(h * RET_DV, (h + 1) * RET_DV)
        qr = rotated(q_ref, h)
        kr = rotated(k_ref, h)
        qb = qr.astype(BF16)
        kb = kr.astype(BF16)
        v = v_ref[:, cols]
        s = lax.dot_general(qb, kb, (((1,), (1,)), ((), ())), preferred_element_type=F32) * dm_ref[h]
        o = jnp.dot(s.astype(BF16), v, preferred_element_type=F32)
        state = state_ref[h]
        o = o + jnp.dot(qb, state.astype(BF16), preferred_element_type=F32) * qd_ref[h]
        kdec = (kr * kd_ref[h]).astype(BF16)
        state_ref[h] = state * cd_ref[h] + lax.dot_general(
            kdec, v, (((0,), (0,)), ((), ())), preferred_element_type=F32)
        r = lax.rsqrt(jnp.mean(o * o, axis=-1, keepdims=True) + EPS)
        g = g_ref[:, cols].astype(F32)
        o_ref[:, cols] = (o * r * og_ref[h] * _silu(g)).astype(o_ref.dtype)


def _retention(p_main, tables, ret_out_g, layer, B, S):
    T = p_main.shape[0]
    W = RET_HEADS * RET_DV
    R = tables[2].shape[1]
    nr = S // R
    cos, sin, dm, qd, kd, cd = tables
    row = lambda b, i: b * nr + i
    full3 = lambda a: pl.BlockSpec(a.shape, lambda b, i: (0, 0, 0))
    return pl.pallas_call(
        _retention_kernel,
        grid=(B, nr),
        in_specs=[pl.BlockSpec((R, W), lambda b, i: (row(b, i), 0)),
                  pl.BlockSpec((R, W), lambda b, i: (row(b, i), 1)),
                  pl.BlockSpec((R, W), lambda b, i: (row(b, i), 2)),
                  pl.BlockSpec((R, W), lambda b, i: (row(b, i), 3)),
                  pl.BlockSpec((R, RET_DK // 2), lambda b, i: (i, 0)),
                  pl.BlockSpec((R, RET_DK // 2), lambda b, i: (i, 0)),
                  full3(dm), full3(qd), full3(kd), full3(cd),
                  pl.BlockSpec((None, RET_HEADS, 1, RET_DV), lambda b, i: (layer, 0, 0, 0))],
        out_specs=pl.BlockSpec((R, W), lambda b, i: (row(b, i), 0)),
        out_shape=jax.ShapeDtypeStruct((T, W), BF16),
        scratch_shapes=[pltpu.VMEM((RET_HEADS, RET_DK, RET_DV), F32)],
        compiler_params=_params(2),
        name="retention",
    )(p_main, p_main, p_main, p_main, cos, sin, dm, qd, kd, cd, ret_out_g)


def _prepare_keys_kernel(ak_ref, av_ref, ik_ref, gk_ref, gik_ref, kn_ref, vt_ref, ikn_ref):
    ak = ak_ref[...].astype(F32)
    r = lax.rsqrt(jnp.mean(ak * ak, axis=-1, keepdims=True) + EPS)
    kn_ref[...] = (ak * r * gk_ref[...]).astype(BF16)
    vt_ref[...] = av_ref[...].astype(F32).T.astype(BF16)
    ik = ik_ref[...].astype(F32)
    ik = jnp.where(lax.broadcasted_iota(jnp.int32, ik.shape, 1) < IDX_DH, ik, 0.0)
    r = lax.rsqrt(jnp.sum(ik * ik, axis=-1, keepdims=True) * (1.0 / IDX_DH) + EPS)
    ikn_ref[...] = (ik * r * gik_ref[...]).astype(BF16)


def _prepare_keys(p_small, gk, gik, layer, B, S):
    seg = IDX_HEADS * IDX_DH // LANES
    return pl.pallas_call(
        _prepare_keys_kernel,
        grid=(B,),
        in_specs=[pl.BlockSpec((S, LANES), lambda b: (b, seg)),
                  pl.BlockSpec((S, LANES), lambda b: (b, seg + 1)),
                  pl.BlockSpec((S, LANES), lambda b: (b, seg + 2)),
                  pl.BlockSpec((None, 1, ATT_DH), lambda b: (layer, 0, 0)),
                  pl.BlockSpec((None, 1, LANES), lambda b: (layer, 0, 0))],
        out_specs=[pl.BlockSpec((None, S, ATT_DH), lambda b: (b, 0, 0)),
                   pl.BlockSpec((None, ATT_DH, S), lambda b: (b, 0, 0)),
                   pl.BlockSpec((None, S, LANES), lambda b: (b, 0, 0))],
        out_shape=[jax.ShapeDtypeStruct((B, S, ATT_DH), BF16),
                   jax.ShapeDtypeStruct((B, ATT_DH, S), BF16),
                   jax.ShapeDtypeStruct((B, S, LANES), BF16)],
        compiler_params=_params(1),
        name="prepare_keys",
    )(p_small, p_small, p_small, gk, gik)


def _sparse_attention_kernel(iq_ref, iw_ref, aq_ref, ag_ref, kn_ref, vt_ref, ikn_ref, gq_ref, *rest,
                             topk, first_block):
    o_ref, key_ref, bias_ref, qt_ref, s_ref, p_ref, ot_ref = rest[-7:]
    S = kn_ref.shape[0]
    TQ = Q_BLOCK
    n_sub = iq_ref.shape[0] // TQ
    n_bits = 31
    blocks = [slice(c * KEY_ROWS, (c + 1) * KEY_ROWS) for c in range(S // KEY_ROWS)]

    def query_rows(sub):
        return pl.ds(pl.multiple_of(sub * TQ, TQ), TQ)

    def scores(sub):
        q_rows = query_rows(sub)
        iqT = iq_ref[q_rows, :].astype(F32).T.astype(BF16)
        wT = iw_ref[q_rows, :].astype(F32).T[IDX_DH:IDX_DH + IDX_HEADS, :] * (IDX_HEADS ** -0.5 * IDX_DH ** -0.5)
        no_rows = jnp.zeros((LANES - IDX_DH, 2 * TQ), BF16)
        score = jnp.zeros((S, TQ), F32)
        for hp in range(IDX_HEADS // 2):
            h0, h1 = 2 * hp, 2 * hp + 1
            q_pair = jnp.concatenate([iqT[h0 * IDX_DH:(h0 + 1) * IDX_DH, :],
                                      iqT[h1 * IDX_DH:(h1 + 1) * IDX_DH, :]], axis=1)
            s = jnp.dot(ikn_ref[...], jnp.concatenate([q_pair, no_rows], axis=0), preferred_element_type=F32)
            score = score + jnp.maximum(s[:, :TQ], 0.0) * wT[h0:h0 + 1, :]
            score = score + jnp.maximum(s[:, TQ:], 0.0) * wT[h1:h1 + 1, :]
        bits = lax.bitcast_convert_type(score, jnp.int32)
        key = bits ^ ((bits >> 31) & 0x7FFFFFFF)
        first_query = (first_block + pl.program_id(1) * n_sub + sub) * TQ
        kpos = lax.broadcasted_iota(jnp.int32, (S, TQ), 0)
        qpos = lax.broadcasted_iota(jnp.int32, (S, TQ), 1) + first_query
        key_ref[sub] = jnp.where(kpos // CHUNK <= qpos // CHUNK, key, INT_MIN)

    def count(sub, pred):
        acc = jnp.zeros((KEY_ROWS, TQ), F32)
        for c, rows in enumerate(blocks):
            kpos = lax.broadcasted_iota(jnp.int32, (KEY_ROWS, TQ), 0) + c * KEY_ROWS
            acc = acc + jnp.where(pred(key_ref[sub, rows, :], kpos), 1.0, 0.0)
        return _reduce_rows(acc, jnp.add, jnp.sum)

    def try_threshold(sub, cand, thr, cnt):
        c = count(sub, lambda key, kpos: key >= cand)
        ok = c >= topk
        return jnp.where(ok, cand, thr), jnp.where(ok, c, cnt)

    def sign_step(sub):
        thr = jnp.full((1, TQ), INT_MIN, jnp.int32)
        cnt = jnp.full((1, TQ), float(S), F32)
        return try_threshold(sub, jnp.zeros((1, TQ), jnp.int32), thr, cnt)

    def bit_step(sub, it, thr, cnt):
        return try_threshold(sub, thr | (1 << (n_bits - 1 - it)), thr, cnt)

    def selected_bias(sub, selected):
        for c, rows in enumerate(blocks):
            key = key_ref[sub, rows, :]
            kpos = lax.broadcasted_iota(jnp.int32, (KEY_ROWS, TQ), 0) + c * KEY_ROWS
            finite = (key > KEY_NEG_INF) & (key < KEY_POS_INF)
            b = jnp.where(selected(key, kpos) & finite, 0.0, MASKED_LOGIT)
            bias_ref[sub, rows, :] = b.astype(BF16)

    def select(sub, thr, cnt):
        selected_bias(sub, lambda key, kpos: key >= thr)
        tied = jnp.where((cnt > topk) & (thr > KEY_NEG_INF), 1.0, 0.0)

        @pl.when(jnp.max(tied) > 0.0)
        def _break_ties():
            need = topk - count(sub, lambda key, kpos: key > thr)

            def pos_step(it, bound):
                cand = bound | (1 << (S.bit_length() - 1 - it))
                ok = count(sub, lambda key, kpos: (key == thr) & (kpos < cand)) <= need
                return jnp.where(ok, cand, bound)

            bound = lax.fori_loop(0, S.bit_length(), pos_step, jnp.zeros((1, TQ), jnp.int32))
            selected_bias(sub, lambda key, kpos: (key > thr) | ((key == thr) & (kpos < bound)))

    def normalise_queries(sub):
        aqT = aq_ref[query_rows(sub), :].astype(F32).T
        gq = gq_ref[...] * (ATT_DH ** -0.5 * LOG2_E)
        for h in range(ATT_HEADS):
            x = aqT[h * ATT_DH:(h + 1) * ATT_DH, :]
            r = lax.rsqrt(_reduce_rows(x * x, jnp.add, jnp.sum) * (1.0 / ATT_DH) + EPS)
            qt_ref[sub, h * ATT_DH:(h + 1) * ATT_DH, :] = (x * r * gq).astype(BF16)

    n_pairs = ATT_HEADS // 2
    eye = (lax.broadcasted_iota(jnp.int32, (TQ, TQ), 0) == lax.broadcasted_iota(jnp.int32, (TQ, TQ), 1))
    eye = jnp.where(eye, 1.0, 0.0).astype(BF16)
    eye_pair = jnp.concatenate([eye, eye], axis=1)

    def attention(sub):
        def logits_stage(hp, rows, q_pair, top):
            s = jnp.dot(jnp.concatenate([kn_ref[rows, :], bias_ref[sub, rows, :]], axis=1),
                        jnp.concatenate([q_pair, eye_pair], axis=0), preferred_element_type=F32)
            s_ref[hp % 2, rows, :] = s
            return jnp.maximum(top, _fold_rows(s, jnp.maximum))

        def probs_stage(hp, rows, top, den):
            p = jnp.exp2(s_ref[hp % 2, rows, :] - top)
            p_ref[hp % 2, rows, :] = p.astype(BF16)
            return den + _fold_rows(p, jnp.add)

        def values_stage(hp, rows, acc):
            return acc + jnp.dot(vt_ref[:, rows], p_ref[hp % 2, rows, :], preferred_element_type=F32)

        def query_pair(hp):
            row0 = hp * 2 * ATT_DH
            return jnp.concatenate([qt_ref[sub, row0:row0 + ATT_DH, :],
                                    qt_ref[sub, row0 + ATT_DH:row0 + 2 * ATT_DH, :]], axis=1)

        def finish(hp, acc, den):
            oT = acc / den
            row0 = hp * 2 * ATT_DH
            ot_ref[sub, row0:row0 + ATT_DH, :] = oT[:, :TQ]
            ot_ref[sub, row0 + ATT_DH:row0 + 2 * ATT_DH, :] = oT[:, TQ:]

        neg = jnp.full((8, 2 * TQ), -jnp.inf, F32)
        q_next = query_pair(0)
        top_next = neg
        for rows in blocks:
            top_next = logits_stage(0, rows, q_next, top_next)
        den_prev = None
        for hp in range(n_pairs):
            top = jnp.max(top_next, axis=0, keepdims=True)
            if hp + 1 < n_pairs:
                q_next = query_pair(hp + 1)
                top_next = neg
            den = jnp.zeros((8, 2 * TQ), F32)
            acc = jnp.zeros((ATT_DH, 2 * TQ), F32)
            for rows in blocks:
                if hp + 1 < n_pairs:
                    top_next = logits_stage(hp + 1, rows, q_next, top_next)
                den = probs_stage(hp, rows, top, den)
                if hp > 0:
                    acc = values_stage(hp - 1, rows, acc)
            if hp > 0:
                finish(hp - 1, acc, den_prev)
            den_prev = jnp.sum(den, axis=0, keepdims=True)
        acc = jnp.zeros((ATT_DH, 2 * TQ), F32)
        for rows in blocks:
            acc = values_stage(n_pairs - 1, rows, acc)
        finish(n_pairs - 1, acc, den_prev)

    def query_block(sub, carry):
        scores(sub)
        thr, cnt = lax.fori_loop(0, n_bits, lambda it, found: bit_step(sub, it, *found), sign_step(sub))
        select(sub, thr, cnt)
        normalise_queries(sub)
        attention(sub)
        q_rows = query_rows(sub)
        o_ref[q_rows, :] = (ot_ref[sub].T * _silu(ag_ref[q_rows, :].astype(F32))).astype(o_ref.dtype)
        return carry

    lax.fori_loop(0, n_sub, query_block, 0)


def _sparse_attention(p_main, p_small, keys, gq, out, layer, B, S):
    T = p_main.shape[0]
    W = ATT_HEADS * ATT_DH
    TQ = Q_BLOCK
    nb = S // TQ
    topk = min(INDEX_TOPK, S // 4)
    n_iq = IDX_HEADS * IDX_DH
    seg = n_iq // LANES
    kn, vt, ikn = keys
    group_keys = max(KEY_ROWS, S // 8)
    group_blocks = group_keys // TQ
    n_sub = 2
    assert group_blocks % n_sub == 0
    rows = n_sub * TQ
    assert out.shape == (T, W) and out.dtype == BF16
    for first_block in range(0, nb, group_blocks):
        n_keys = (first_block + group_blocks) * TQ
        qrow = lambda b, i, first_block=first_block: (b * nb + first_block) // n_sub + i
        in_specs = [pl.BlockSpec((rows, n_iq), lambda b, i: (qrow(b, i), 0)),
                    pl.BlockSpec((rows, LANES), lambda b, i: (qrow(b, i), seg + 2)),
                    pl.BlockSpec((rows, W), lambda b, i: (qrow(b, i), 4)),
                    pl.BlockSpec((rows, W), lambda b, i: (qrow(b, i), 5)),
                    pl.BlockSpec((None, n_keys, ATT_DH), lambda b, i: (b, 0, 0)),
                    pl.BlockSpec((None, ATT_DH, n_keys), lambda b, i: (b, 0, 0)),
                    pl.BlockSpec((None, n_keys, LANES), lambda b, i: (b, 0, 0)),
                    pl.BlockSpec((None, ATT_DH, 1), lambda b, i: (layer, 0, 0)),
                    pl.BlockSpec(memory_space=pl.ANY)]
        args = [p_small, p_small, p_main, p_main, kn, vt, ikn, gq, out]
        out = pl.pallas_call(
            functools.partial(_sparse_attention_kernel, topk=topk, first_block=first_block),
            grid=(B, group_blocks // n_sub),
            in_specs=in_specs,
            out_specs=pl.BlockSpec((rows, W), lambda b, i: (qrow(b, i), 0)),
            out_shape=jax.ShapeDtypeStruct((T, W), BF16),
            scratch_shapes=[pltpu.VMEM((n_sub, n_keys, TQ), jnp.int32),
                            pltpu.VMEM((n_sub, n_keys, TQ), BF16),
                            pltpu.VMEM((n_sub, W, TQ), BF16),
                            pltpu.VMEM((2, n_keys, 2 * TQ), F32),
                            pltpu.VMEM((2, n_keys, 2 * TQ), BF16),
                            pltpu.VMEM((n_sub, W, TQ), F32)],
            input_output_aliases={len(args) - 1: 0},
            compiler_params=_params(2),
            name="sparse_attention",
        )(*args)
    return out


def _merge_kernel(ar_ref, aa_ref, wr_ref, wa_ref, ga_ref, gb_ref, m_ref, wrb_ref, wab_ref):
    @pl.when(pl.program_id(1) == 0)
    def _():
        wrb_ref[...] = wr_ref[...].astype(BF16)
        wab_ref[...] = wa_ref[...].astype(BF16)

    u_ret = jnp.dot(ar_ref[...], wrb_ref[...], preferred_element_type=F32)
    u_att = jnp.dot(aa_ref[...], wab_ref[...], preferred_element_type=F32)
    m = _sigmoid(ga_ref[...].astype(F32)) * u_ret + _sigmoid(gb_ref[...].astype(F32)) * u_att
    m_ref[...] = m.astype(m_ref.dtype)


def _merge(a_ret, a_att, w_ret, w_att, p_main, layer):
    T, K = a_ret.shape
    D = w_ret.shape[-1]
    tm = _tile(T, 512)
    tn = _tile(D, 1024)
    gate_a = 6 * (D // tn)
    gate_b = 7 * (D // tn)
    w_spec = pl.BlockSpec((None, K, tn), lambda j, i: (layer, 0, j), pipeline_mode=pl.Buffered(1))
    return pl.pallas_call(
        _merge_kernel,
        grid=(D // tn, T // tm),
        in_specs=[pl.BlockSpec((tm, K), lambda j, i: (i, 0)),
                  pl.BlockSpec((tm, K), lambda j, i: (i, 0)),
                  w_spec, w_spec,
                  pl.BlockSpec((tm, tn), lambda j, i: (i, gate_a + j)),
                  pl.BlockSpec((tm, tn), lambda j, i: (i, gate_b + j))],
        out_specs=pl.BlockSpec((tm, tn), lambda j, i: (i, j)),
        out_shape=jax.ShapeDtypeStruct((T, D), BF16),
        scratch_shapes=[pltpu.VMEM((K, tn), BF16), pltpu.VMEM((K, tn), BF16)],
        compiler_params=_params(2),
        name="merge",
    )(a_ret, a_att, w_ret, w_att, p_main, p_main)


def _output_kernel(m_ref, w_ref, x_ref, *refs):
    wb_ref = refs[-1]

    @pl.when(pl.program_id(0) == 0)
    def _():
        wb_ref[...] = w_ref[...].astype(BF16)

    x = x_ref[...] + jnp.dot(m_ref[...], wb_ref[...], preferred_element_type=F32)
    if len(refs) == 2:
        refs[0][...] = x
    else:
        g_ref, xo_ref, h_ref = refs[:3]
        xo_ref[...] = x
        r = lax.rsqrt(jnp.mean(x * x, axis=-1, keepdims=True) + EPS)
        h_ref[...] = (x * r * g_ref[...]).astype(h_ref.dtype)


def _output(m, w_out, x, norm_g, layer, last):
    T, D = x.shape
    tm = _tile(T, 256)
    row_spec = pl.BlockSpec((tm, D), lambda i: (i, 0))
    w_spec = pl.BlockSpec((None, D, D), lambda i: (layer, 0, 0), pipeline_mode=pl.Buffered(1))
    in_specs = [row_spec, w_spec, row_spec]
    scratch = [pltpu.VMEM((D, D), BF16)]
    if last:
        return pl.pallas_call(
            _output_kernel, grid=(T // tm,), in_specs=in_specs, out_specs=row_spec,
            out_shape=jax.ShapeDtypeStruct((T, D), F32), scratch_shapes=scratch,
            compiler_params=_params(1), name="output_last",
        )(m, w_out, x), None
    return pl.pallas_call(
        _output_kernel, grid=(T // tm,),
        in_specs=in_specs + [pl.BlockSpec((None, 1, D), lambda i: (layer + 1, 0, 0))],
        out_specs=[row_spec, row_spec],
        out_shape=[jax.ShapeDtypeStruct((T, D), F32), jax.ShapeDtypeStruct((T, D), BF16)],
        scratch_shapes=scratch, compiler_params=_params(1), name="output",
    )(m, w_out, x, norm_g)


def kernel(x, norm_g, w_in, ret_out_g, att_q_g, att_k_g, idx_k_g, w_branch_ret, w_branch_att, w_out):
    B, S, D = x.shape
    depth = w_in.shape[0]
    T = B * S
    assert w_in.shape[-1] == N_IN and S % KEY_ROWS == 0
    norm_g3 = norm_g.reshape(depth, 1, D)
    ret_g4 = ret_out_g.reshape(depth, RET_HEADS, 1, RET_DV)
    gq = att_q_g.reshape(depth, ATT_DH, 1)
    gk = att_k_g.reshape(depth, 1, ATT_DH)
    gik = jnp.pad(idx_k_g, ((0, 0), (0, LANES - IDX_DH))).reshape(depth, 1, LANES)
    tables = _retention_tables(S, _tile(S, 256))
    w_t = jnp.swapaxes(w_in, 1, 2)

    xf = x.reshape(T, D)
    h = _rmsnorm(xf, norm_g3, 0)
    for layer in range(depth):
        p_main = _project_main(h, w_t, layer)
        p_small = _project_small(h, w_t, layer)
        a_ret = _retention(p_main, tables, ret_g4, layer, B, S)
        keys = _prepare_keys(p_small, gk, gik, layer, B, S)
        a_att = _sparse_attention(p_main, p_small, keys, gq, h, layer, B, S)
        m = _merge(a_ret, a_att, w_branch_ret, w_branch_att, p_main, layer)
        xf, h = _output(m, w_out, xf, norm_g3, layer, last=layer == depth - 1)
    return xf.reshape(B, S, D)
```

```python
import functools
import math

import jax
import jax.numpy as jnp
from jax import lax
from jax.experimental import pallas as pl
from jax.experimental.pallas import tpu as pltpu

F32 = jnp.float32
BF16 = jnp.bfloat16

CHUNK = 64
EPS = 1e-6
RET_HEADS = 8
RET_DK = 256
RET_DV = 256
ROT_BASE = 10000.0
ATT_HEADS = 16
ATT_DH = 128
IDX_HEADS = 16
IDX_DH = 64
INDEX_TOPK = 256
Q_BLOCK = 128
LANES = 128
KEY_ROWS = 256

INT_MIN = -(2 ** 31)
KEY_NEG_INF = -2139095041
KEY_POS_INF = 2139095040
MASKED_LOGIT = -1e30
LOG2_E = 1.4426950408889634

VMEM_LIMIT_BYTES = 56 * 1024 * 1024


def _params(n_axes):
    return pltpu.CompilerParams(dimension_semantics=("arbitrary",) * n_axes,
                                vmem_limit_bytes=VMEM_LIMIT_BYTES)


def _tile(n, pref):
    t = min(n, pref)
    assert n % t == 0, (n, t)
    return t


def _sigmoid(g):
    return 0.5 * jnp.tanh(0.5 * g) + 0.5


def _silu(g):
    half = 0.5 * g
    return half * jnp.tanh(half) + half


def _fold_rows(x, op):
    SUBLANES = 8
    while x.shape[0] > SUBLANES and x.shape[0] % (2 * SUBLANES) == 0:
        half = x.shape[0] // 2
        x = op(x[:half], x[half:])
    return x


def _reduce_rows(x, op, finish):
    return finish(_fold_rows(x, op), axis=0, keepdims=True)


def _rmsnorm_kernel(x_ref, g_ref, h_ref):
    x = x_ref[...]
    r = lax.rsqrt(jnp.mean(x * x, axis=-1, keepdims=True) + EPS)
    h_ref[...] = (x * r * g_ref[...]).astype(h_ref.dtype)


def _rmsnorm(x, norm_g, layer):
    T, D = x.shape
    tm = _tile(T, 512)
    return pl.pallas_call(
        _rmsnorm_kernel,
        grid=(T // tm,),
        in_specs=[pl.BlockSpec((tm, D), lambda i: (i, 0)),
                  pl.BlockSpec((None, 1, D), lambda i: (layer, 0, 0))],
        out_specs=pl.BlockSpec((tm, D), lambda i: (i, 0)),
        out_shape=jax.ShapeDtypeStruct((T, D), BF16),
        compiler_params=_params(1),
        name="rmsnorm",
    )(x, norm_g)


OFF_AQ = 4 * RET_HEADS * RET_DK
OFF_AK = OFF_AQ + ATT_HEADS * ATT_DH
OFF_AV = OFF_AK + ATT_DH
OFF_AG = OFF_AV + ATT_DH
OFF_IQ = OFF_AG + ATT_HEADS * ATT_DH
OFF_IK = OFF_IQ + IDX_HEADS * IDX_DH
OFF_IW = OFF_IK + IDX_DH
OFF_GA = OFF_IW + IDX_HEADS
N_IN = OFF_GA + 2 * RET_HEADS * RET_DK


def _project_kernel(a_ref, *refs):
    *w_refs, o_ref, wb_ref = refs
    K = wb_ref.shape[0]
    STRIP = 512

    @pl.when(pl.program_id(1) == 0)
    def _():
        for k0 in range(0, K, STRIP):
            parts = [r[0, :, k0:k0 + STRIP] for r in w_refs]
            w = parts[0] if len(parts) == 1 else jnp.concatenate(parts, axis=0)
            wb_ref[k0:k0 + STRIP, :] = w.T.astype(BF16)

    o_ref[...] = jnp.dot(a_ref[...], wb_ref[...], preferred_element_type=F32).astype(o_ref.dtype)


def _project(h, w_t, windows, n_tiles, name):
    T, K = h.shape
    tm = _tile(T, 1024)
    tn = sum(rows for rows, _ in windows)
    specs = [pl.BlockSpec((pl.Element(1), pl.Element(rows), pl.Element(K)), index_map)
             for rows, index_map in windows]
    return pl.pallas_call(
        _project_kernel,
        grid=(n_tiles, T // tm),
        in_specs=[pl.BlockSpec((tm, K), lambda j, i: (i, 0))] + specs,
        out_specs=pl.BlockSpec((tm, tn), lambda j, i: (i, j)),
        out_shape=jax.ShapeDtypeStruct((T, n_tiles * tn), BF16),
        scratch_shapes=[pltpu.VMEM((K, tn), BF16)],
        compiler_params=_params(2),
        name=name,
    )(h, *([w_t] * len(specs)))


def _project_main(h, w_t, layer):
    tn = 1024
    SUB = 8
    runs = ((0, OFF_AK), (OFF_AG, OFF_IQ), (OFF_GA, N_IN))
    assert all(a % SUB == 0 and (b - a) % tn == 0 for a, b in runs)
    tiles_before = [sum((b - a) // tn for a, b in runs[:r]) for r in range(len(runs) + 1)]

    def first_row(j):
        row = j * (tn // SUB)
        for r in range(1, len(runs)):
            gap = runs[r][0] - runs[r - 1][1]
            row = row + jnp.where(j >= tiles_before[r], gap // SUB, 0)
        return row * SUB

    return _project(h, w_t, [(tn, lambda j, i: (layer, first_row(j), 0))], tiles_before[-1], "project_main")


def _project_small(h, w_t, layer):
    windows = [(IDX_HEADS * IDX_DH, lambda j, i: (layer, OFF_IQ, 0)),
               (2 * ATT_DH, lambda j, i: (layer, OFF_AK, 0)),
               (LANES, lambda j, i: (layer, OFF_IK, 0))]
    return _project(h, w_t, windows, 1, "project_small")


def _retention_tables(S, R):
    pos = jnp.arange(S, dtype=F32)
    inv = 1.0 / (ROT_BASE ** jnp.linspace(0.0, 1.0, RET_DK // 2, dtype=F32))
    ang = pos[:, None] * inv[None, :]
    log_g = jnp.log(1.0 - 2.0 ** (-5.0 - jnp.arange(RET_HEADS, dtype=F32)))[:, None, None]
    idx = jnp.arange(R, dtype=F32)
    chunk = jnp.arange(R) // CHUNK
    visible = chunk[None, :] <= chunk[:, None]
    k_scale = RET_DK ** -0.5
    assert k_scale == 2.0 ** round(math.log2(k_scale))
    dm = jnp.where(visible[None], jnp.exp(jnp.abs(idx[:, None] - idx[None, :])[None] * log_g), 0.0) * k_scale
    ones = jnp.ones((1, 1, RET_DV), F32)
    qd = jnp.exp((idx[None, :, None] + 1.0) * log_g) * ones
    kd = jnp.exp((R - 1.0 - idx)[None, :, None] * log_g) * ones * k_scale
    cd = jnp.exp(float(R) * log_g) * ones
    return jnp.cos(ang), jnp.sin(ang), dm, qd, kd, cd


def _retention_kernel(q_ref, k_ref, v_ref, g_ref, cos_ref, sin_ref, dm_ref, qd_ref, kd_ref, cd_ref,
                      og_ref, o_ref, state_ref):
    @pl.when(pl.program_id(1) == 0)
    def _():
        state_ref[...] = jnp.zeros_like(state_ref)

    cos = cos_ref[...]
    sin = sin_ref[...]
    half = RET_DK // 2

    def rotated(ref, h):
        x1 = ref[:, h * RET_DK:h * RET_DK + half].astype(F32)
        x2 = ref[:, h * RET_DK + half:(h + 1) * RET_DK].astype(F32)
        return jnp.concatenate([x1 * cos - x2 * sin, x2 * cos + x1 * sin], axis=1)

    for h in range(RET_HEADS):
        cols = slice(h * RET_DV, (h + 1) * RET_DV)
        qr = rotated(q_ref, h)
        kr = rotated(k_ref, h)
        qb = qr.astype(BF16)
        kb = kr.astype(BF16)
        v = v_ref[:, cols]
        s = lax.dot_general(qb, kb, (((1,), (1,)), ((), ())), preferred_element_type=F32) * dm_ref[h]
        o = jnp.dot(s.astype(BF16), v, preferred_element_type=F32)
        state = state_ref[h]
        o = o + jnp.dot(qb, state.astype(BF16), preferred_element_type=F32) * qd_ref[h]
        kdec = (kr * kd_ref[h]).astype(BF16)
        state_ref[h] = state * cd_ref[h] + lax.dot_general(
            kdec, v, (((0,), (0,)), ((), ())), preferred_element_type=F32)
        r = lax.rsqrt(jnp.mean(o * o, axis=-1, keepdims=True) + EPS)
        g = g_ref[:, cols].astype(F32)
        o_ref[:, cols] = (o * r * og_ref[h] * _silu(g)).astype(o_ref.dtype)


def _retention(p_main, tables, ret_out_g, layer, B, S):
    T = p_main.shape[0]
    W = RET_HEADS * RET_DV
    R = tables[2].shape[1]
    nr = S // R
    cos, sin, dm, qd, kd, cd = tables
    row = lambda b, i: b * nr + i
    full3 = lambda a: pl.BlockSpec(a.shape, lambda b, i: (0, 0, 0))
    return pl.pallas_call(
        _retention_kernel,
        grid=(B, nr),
        in_specs=[pl.BlockSpec((R, W), lambda b, i: (row(b, i), 0)),
                  pl.BlockSpec((R, W), lambda b, i: (row(b, i), 1)),
                  pl.BlockSpec((R, W), lambda b, i: (row(b, i), 2)),
                  pl.BlockSpec((R, W), lambda b, i: (row(b, i), 3)),
                  pl.BlockSpec((R, RET_DK // 2), lambda b, i: (i, 0)),
                  pl.BlockSpec((R, RET_DK // 2), lambda b, i: (i, 0)),
                  full3(dm), full3(qd), full3(kd), full3(cd),
                  pl.BlockSpec((None, RET_HEADS, 1, RET_DV), lambda b, i: (layer, 0, 0, 0))],
        out_specs=pl.BlockSpec((R, W), lambda b, i: (row(b, i), 0)),
        out_shape=jax.ShapeDtypeStruct((T, W), BF16),
        scratch_shapes=[pltpu.VMEM((RET_HEADS, RET_DK, RET_DV), F32)],
        compiler_params=_params(2),
        name="retention",
    )(p_main, p_main, p_main, p_main, cos, sin, dm, qd, kd, cd, ret_out_g)


def _prepare_keys_kernel(ak_ref, av_ref, ik_ref, gk_ref, gik_ref, kn_ref, vt_ref, ikn_ref):
    ak = ak_ref[...].astype(F32)
    r = lax.rsqrt(jnp.mean(ak * ak, axis=-1, keepdims=True) + EPS)
    kn_ref[...] = (ak * r * gk_ref[...]).astype(BF16)
    vt_ref[...] = av_ref[...].astype(F32).T.astype(BF16)
    ik = ik_ref[...].astype(F32)
    ik = jnp.where(lax.broadcasted_iota(jnp.int32, ik.shape, 1) < IDX_DH, ik, 0.0)
    r = lax.rsqrt(jnp.sum(ik * ik, axis=-1, keepdims=True) * (1.0 / IDX_DH) + EPS)
    ikn_ref[...] = (ik * r * gik_ref[...]).astype(BF16)


def _prepare_keys(p_small, gk, gik, layer, B, S):
    seg = IDX_HEADS * IDX_DH // LANES
    return pl.pallas_call(
        _prepare_keys_kernel,
        grid=(B,),
        in_specs=[pl.BlockSpec((S, LANES), lambda b: (b, seg)),
                  pl.BlockSpec((S, LANES), lambda b: (b, seg + 1)),
                  pl.BlockSpec((S, LANES), lambda b: (b, seg + 2)),
                  pl.BlockSpec((None, 1, ATT_DH), lambda b: (layer, 0, 0)),
                  pl.BlockSpec((None, 1, LANES), lambda b: (layer, 0, 0))],
        out_specs=[pl.BlockSpec((None, S, ATT_DH), lambda b: (b, 0, 0)),
                   pl.BlockSpec((None, ATT_DH, S), lambda b: (b, 0, 0)),
                   pl.BlockSpec((None, S, LANES), lambda b: (b, 0, 0))],
        out_shape=[jax.ShapeDtypeStruct((B, S, ATT_DH), BF16),
                   jax.ShapeDtypeStruct((B, ATT_DH, S), BF16),
                   jax.ShapeDtypeStruct((B, S, LANES), BF16)],
        compiler_params=_params(1),
        name="prepare_keys",
    )(p_small, p_small, p_small, gk, gik)


def _sparse_attention_kernel(iq_ref, iw_ref, aq_ref, ag_ref, kn_ref, vt_ref, ikn_ref, gq_ref, *rest,
                             topk, first_block):
    o_ref, key_ref, bias_ref, qt_ref, s_ref, p_ref, ot_ref = rest[-7:]
    S = kn_ref.shape[0]
    TQ = Q_BLOCK
    n_sub = iq_ref.shape[0] // TQ
    n_bits = 31
    blocks = [slice(c * KEY_ROWS, (c + 1) * KEY_ROWS) for c in range(S // KEY_ROWS)]

    def scores(sub):
        q_rows = slice(sub * TQ, (sub + 1) * TQ)
        iqT = iq_ref[q_rows, :].astype(F32).T.astype(BF16)
        wT = iw_ref[q_rows, :].astype(F32).T[IDX_DH:IDX_DH + IDX_HEADS, :] * (IDX_HEADS ** -0.5 * IDX_DH ** -0.5)
        no_rows = jnp.zeros((LANES - IDX_DH, 2 * TQ), BF16)
        score = jnp.zeros((S, TQ), F32)
        for hp in range(IDX_HEADS // 2):
            h0, h1 = 2 * hp, 2 * hp + 1
            q_pair = jnp.concatenate([iqT[h0 * IDX_DH:(h0 + 1) * IDX_DH, :],
                                      iqT[h1 * IDX_DH:(h1 + 1) * IDX_DH, :]], axis=1)
            s = jnp.dot(ikn_ref[...], jnp.concatenate([q_pair, no_rows], axis=0), preferred_element_type=F32)
            score = score + jnp.maximum(s[:, :TQ], 0.0) * wT[h0:h0 + 1, :]
            score = score + jnp.maximum(s[:, TQ:], 0.0) * wT[h1:h1 + 1, :]
        bits = lax.bitcast_convert_type(score, jnp.int32)
        key = bits ^ ((bits >> 31) & 0x7FFFFFFF)
        first_query = (first_block + pl.program_id(1) * n_sub + sub) * TQ
        kpos = lax.broadcasted_iota(jnp.int32, (S, TQ), 0)
        qpos = lax.broadcasted_iota(jnp.int32, (S, TQ), 1) + first_query
        key_ref[sub] = jnp.where(kpos // CHUNK <= qpos // CHUNK, key, INT_MIN)

    def count(sub, pred):
        acc = jnp.zeros((KEY_ROWS, TQ), F32)
        for c, rows in enumerate(blocks):
            kpos = lax.broadcasted_iota(jnp.int32, (KEY_ROWS, TQ), 0) + c * KEY_ROWS
            acc = acc + jnp.where(pred(key_ref[sub, rows, :], kpos), 1.0, 0.0)
        return _reduce_rows(acc, jnp.add, jnp.sum)

    def try_threshold(sub, cand, thr, cnt):
        c = count(sub, lambda key, kpos: key >= cand)
        ok = c >= topk
        return jnp.where(ok, cand, thr), jnp.where(ok, c, cnt)

    def sign_step(sub):
        thr = jnp.full((1, TQ), INT_MIN, jnp.int32)
        cnt = jnp.full((1, TQ), float(S), F32)
        return try_threshold(sub, jnp.zeros((1, TQ), jnp.int32), thr, cnt)

    def bit_step(sub, it, thr, cnt):
        return try_threshold(sub, thr | (1 << (n_bits - 1 - it)), thr, cnt)

    def selected_bias(sub, selected):
        for c, rows in enumerate(blocks):
            key = key_ref[sub, rows, :]
            kpos = lax.broadcasted_iota(jnp.int32, (KEY_ROWS, TQ), 0) + c * KEY_ROWS
            finite = (key > KEY_NEG_INF) & (key < KEY_POS_INF)
            b = jnp.where(selected(key, kpos) & finite, 0.0, MASKED_LOGIT)
            bias_ref[sub, rows, :] = b.astype(BF16)

    def select(sub, thr, cnt):
        selected_bias(sub, lambda key, kpos: key >= thr)
        tied = jnp.where((cnt > topk) & (thr > KEY_NEG_INF), 1.0, 0.0)

        @pl.when(jnp.max(tied) > 0.0)
        def _break_ties():
            need = topk - count(sub, lambda key, kpos: key > thr)

            def pos_step(it, bound):
                cand = bound | (1 << (S.bit_length() - 1 - it))
                ok = count(sub, lambda key, kpos: (key == thr) & (kpos < cand)) <= need
                return jnp.where(ok, cand, bound)

            bound = lax.fori_loop(0, S.bit_length(), pos_step, jnp.zeros((1, TQ), jnp.int32))
            selected_bias(sub, lambda key, kpos: (key > thr) | ((key == thr) & (kpos < bound)))

    def normalise_queries(sub):
        aqT = aq_ref[sub * TQ:(sub + 1) * TQ, :].astype(F32).T
        gq = gq_ref[...] * (ATT_DH ** -0.5 * LOG2_E)
        for h in range(ATT_HEADS):
            x = aqT[h * ATT_DH:(h + 1) * ATT_DH, :]
            r = lax.rsqrt(_reduce_rows(x * x, jnp.add, jnp.sum) * (1.0 / ATT_DH) + EPS)
            qt_ref[sub, h * ATT_DH:(h + 1) * ATT_DH, :] = (x * r * gq).astype(BF16)

    n_pairs = ATT_HEADS // 2
    eye = (lax.broadcasted_iota(jnp.int32, (TQ, TQ), 0) == lax.broadcasted_iota(jnp.int32, (TQ, TQ), 1))
    eye = jnp.where(eye, 1.0, 0.0).astype(BF16)
    eye_pair = jnp.concatenate([eye, eye], axis=1)

    def attention(sub):
        def logits_stage(hp, rows, q_pair, top):
            s = jnp.dot(jnp.concatenate([kn_ref[rows, :], bias_ref[sub, rows, :]], axis=1),
                        jnp.concatenate([q_pair, eye_pair], axis=0), preferred_element_type=F32)
            s_ref[hp % 2, rows, :] = s
            return jnp.maximum(top, _fold_rows(s, jnp.maximum))

        def probs_stage(hp, rows, top, den):
            p = jnp.exp2(s_ref[hp % 2, rows, :] - top)
            p_ref[hp % 2, rows, :] = p.astype(BF16)
            return den + _fold_rows(p, jnp.add)

        def values_stage(hp, rows, acc):
            return acc + jnp.dot(vt_ref[:, rows], p_ref[hp % 2, rows, :], preferred_element_type=F32)

        def query_pair(hp):
            row0 = hp * 2 * ATT_DH
            return jnp.concatenate([qt_ref[sub, row0:row0 + ATT_DH, :],
                                    qt_ref[sub, row0 + ATT_DH:row0 + 2 * ATT_DH, :]], axis=1)

        def finish(hp, acc, den):
            oT = acc / den
            row0 = hp * 2 * ATT_DH
            ot_ref[sub, row0:row0 + ATT_DH, :] = oT[:, :TQ]
            ot_ref[sub, row0 + ATT_DH:row0 + 2 * ATT_DH, :] = oT[:, TQ:]

        neg = jnp.full((8, 2 * TQ), -jnp.inf, F32)
        q_next = query_pair(0)
        top_next = neg
        for rows in blocks:
            top_next = logits_stage(0, rows, q_next, top_next)
        den_prev = None
        for hp in range(n_pairs):
            top = jnp.max(top_next, axis=0, keepdims=True)
            if hp + 1 < n_pairs:
                q_next = query_pair(hp + 1)
                top_next = neg
            den = jnp.zeros((8, 2 * TQ), F32)
            acc = jnp.zeros((ATT_DH, 2 * TQ), F32)
            for rows in blocks:
                if hp + 1 < n_pairs:
                    top_next = logits_stage(hp + 1, rows, q_next, top_next)
                den = probs_stage(hp, rows, top, den)
                if hp > 0:
                    acc = values_stage(hp - 1, rows, acc)
            if hp > 0:
                finish(hp - 1, acc, den_prev)
            den_prev = jnp.sum(den, axis=0, keepdims=True)
        acc = jnp.zeros((ATT_DH, 2 * TQ), F32)
        for rows in blocks:
            acc = values_stage(n_pairs - 1, rows, acc)
        finish(n_pairs - 1, acc, den_prev)

    for sub in range(n_sub):
        scores(sub)
    if S <= topk:
        found = ((jnp.full((1, TQ), INT_MIN, jnp.int32), jnp.full((1, TQ), float(S), F32)),) * n_sub
    else:
        found = tuple(sign_step(sub) for sub in range(n_sub))
        found = lax.fori_loop(
            0, n_bits, lambda it, found: tuple(bit_step(sub, it, *found[sub]) for sub in range(n_sub)), found)
    for sub in range(n_sub):
        select(sub, *found[sub])
        normalise_queries(sub)
    for sub in range(n_sub):
        attention(sub)

    for sub in range(n_sub):
        q_rows = slice(sub * TQ, (sub + 1) * TQ)
        o_ref[q_rows, :] = (ot_ref[sub].T * _silu(ag_ref[q_rows, :].astype(F32))).astype(o_ref.dtype)


def _sparse_attention(p_main, p_small, keys, gq, out, layer, B, S):
    T = p_main.shape[0]
    W = ATT_HEADS * ATT_DH
    TQ = Q_BLOCK
    nb = S // TQ
    topk = min(INDEX_TOPK, S // 4)
    n_iq = IDX_HEADS * IDX_DH
    seg = n_iq // LANES
    kn, vt, ikn = keys
    group_keys = max(KEY_ROWS, S // 8)
    group_blocks = group_keys // TQ
    n_sub = 2
    assert group_blocks % n_sub == 0
    rows = n_sub * TQ
    assert out.shape == (T, W) and out.dtype == BF16
    for first_block in range(0, nb, group_blocks):
        n_keys = (first_block + group_blocks) * TQ
        qrow = lambda b, i, first_block=first_block: (b * nb + first_block) // n_sub + i
        in_specs = [pl.BlockSpec((rows, n_iq), lambda b, i: (qrow(b, i), 0)),
                    pl.BlockSpec((rows, LANES), lambda b, i: (qrow(b, i), seg + 2)),
                    pl.BlockSpec((rows, W), lambda b, i: (qrow(b, i), 4)),
                    pl.BlockSpec((rows, W), lambda b, i: (qrow(b, i), 5)),
                    pl.BlockSpec((None, n_keys, ATT_DH), lambda b, i: (b, 0, 0)),
                    pl.BlockSpec((None, ATT_DH, n_keys), lambda b, i: (b, 0, 0)),
                    pl.BlockSpec((None, n_keys, LANES), lambda b, i: (b, 0, 0)),
                    pl.BlockSpec((None, ATT_DH, 1), lambda b, i: (layer, 0, 0)),
                    pl.BlockSpec(memory_space=pl.ANY)]
        args = [p_small, p_small, p_main, p_main, kn, vt, ikn, gq, out]
        out = pl.pallas_call(
            functools.partial(_sparse_attention_kernel, topk=topk, first_block=first_block),
            grid=(B, group_blocks // n_sub),
            in_specs=in_specs,
            out_specs=pl.BlockSpec((rows, W), lambda b, i: (qrow(b, i), 0)),
            out_shape=jax.ShapeDtypeStruct((T, W), BF16),
            scratch_shapes=[pltpu.VMEM((n_sub, n_keys, TQ), jnp.int32),
                            pltpu.VMEM((n_sub, n_keys, TQ), BF16),
                            pltpu.VMEM((n_sub, W, TQ), BF16),
                            pltpu.VMEM((2, n_keys, 2 * TQ), F32),
                            pltpu.VMEM((2, n_keys, 2 * TQ), BF16),
                            pltpu.VMEM((n_sub, W, TQ), F32)],
            input_output_aliases={len(args) - 1: 0},
            compiler_params=_params(2),
            name="sparse_attention",
        )(*args)
    return out


def _merge_kernel(ar_ref, aa_ref, wr_ref, wa_ref, ga_ref, gb_ref, m_ref, wrb_ref, wab_ref):
    @pl.when(pl.program_id(1) == 0)
    def _():
        wrb_ref[...] = wr_ref[...].astype(BF16)
        wab_ref[...] = wa_ref[...].astype(BF16)

    u_ret = jnp.dot(ar_ref[...], wrb_ref[...], preferred_element_type=F32)
    u_att = jnp.dot(aa_ref[...], wab_ref[...], preferred_element_type=F32)
    m = _sigmoid(ga_ref[...].astype(F32)) * u_ret + _sigmoid(gb_ref[...].astype(F32)) * u_att
    m_ref[...] = m.astype(m_ref.dtype)


def _merge(a_ret, a_att, w_ret, w_att, p_main, layer):
    T, K = a_ret.shape
    D = w_ret.shape[-1]
    tm = _tile(T, 512)
    tn = _tile(D, 1024)
    gate_a = 6 * (D // tn)
    gate_b = 7 * (D // tn)
    w_spec = pl.BlockSpec((None, K, tn), lambda j, i: (layer, 0, j), pipeline_mode=pl.Buffered(1))
    return pl.pallas_call(
        _merge_kernel,
        grid=(D // tn, T // tm),
        in_specs=[pl.BlockSpec((tm, K), lambda j, i: (i, 0)),
                  pl.BlockSpec((tm, K), lambda j, i: (i, 0)),
                  w_spec, w_spec,
                  pl.BlockSpec((tm, tn), lambda j, i: (i, gate_a + j)),
                  pl.BlockSpec((tm, tn), lambda j, i: (i, gate_b + j))],
        out_specs=pl.BlockSpec((tm, tn), lambda j, i: (i, j)),
        out_shape=jax.ShapeDtypeStruct((T, D), BF16),
        scratch_shapes=[pltpu.VMEM((K, tn), BF16), pltpu.VMEM((K, tn), BF16)],
        compiler_params=_params(2),
        name="merge",
    )(a_ret, a_att, w_ret, w_att, p_main, p_main)


def _output_kernel(m_ref, w_ref, x_ref, *refs):
    wb_ref = refs[-1]

    @pl.when(pl.program_id(0) == 0)
    def _():
        wb_ref[...] = w_ref[...].astype(BF16)

    x = x_ref[...] + jnp.dot(m_ref[...], wb_ref[...], preferred_element_type=F32)
    if len(refs) == 2:
        refs[0][...] = x
    else:
        g_ref, xo_ref, h_ref = refs[:3]
        xo_ref[...] = x
        r = lax.rsqrt(jnp.mean(x * x, axis=-1, keepdims=True) + EPS)
        h_ref[...] = (x * r * g_ref[...]).astype(h_ref.dtype)


def _output(m, w_out, x, norm_g, layer, last):
    T, D = x.shape
    tm = _tile(T, 256)
    row_spec = pl.BlockSpec((tm, D), lambda i: (i, 0))
    w_spec = pl.BlockSpec((None, D, D), lambda i: (layer, 0, 0), pipeline_mode=pl.Buffered(1))
    in_specs = [row_spec, w_spec, row_spec]
    scratch = [pltpu.VMEM((D, D), BF16)]
    if last:
        return pl.pallas_call(
            _output_kernel, grid=(T // tm,), in_specs=in_specs, out_specs=row_spec,
            out_shape=jax.ShapeDtypeStruct((T, D), F32), scratch_shapes=scratch,
            compiler_params=_params(1), name="output_last",
        )(m, w_out, x), None
    return pl.pallas_call(
        _output_kernel, grid=(T // tm,),
        in_specs=in_specs + [pl.BlockSpec((None, 1, D), lambda i: (layer + 1, 0, 0))],
        out_specs=[row_spec, row_spec],
        out_shape=[jax.ShapeDtypeStruct((T, D), F32), jax.ShapeDtypeStruct((T, D), BF16)],
        scratch_shapes=scratch, compiler_params=_params(1), name="output",
    )(m, w_out, x, norm_g)


def kernel(x, norm_g, w_in, ret_out_g, att_q_g, att_k_g, idx_k_g, w_branch_ret, w_branch_att, w_out):
    B, S, D = x.shape
    depth = w_in.shape[0]
    T = B * S
    assert w_in.shape[-1] == N_IN and S % KEY_ROWS == 0
    norm_g3 = norm_g.reshape(depth, 1, D)
    ret_g4 = ret_out_g.reshape(depth, RET_HEADS, 1, RET_DV)
    gq = att_q_g.reshape(depth, ATT_DH, 1)
    gk = att_k_g.reshape(depth, 1, ATT_DH)
    gik = jnp.pad(idx_k_g, ((0, 0), (0, LANES - IDX_DH))).reshape(depth, 1, LANES)
    tables = _retention_tables(S, _tile(S, 256))
    w_t = jnp.swapaxes(w_in, 1, 2)

    xf = x.reshape(T, D)
    h = _rmsnorm(xf, norm_g3, 0)
    for layer in range(depth):
        p_main = _project_main(h, w_t, layer)
        p_small = _project_small(h, w_t, layer)
        a_ret = _retention(p_main, tables, ret_g4, layer, B, S)
        keys = _prepare_keys(p_small, gk, gik, layer, B, S)
        a_att = _sparse_attention(p_main, p_small, keys, gq, h, layer, B, S)
        m = _merge(a_ret, a_att, w_branch_ret, w_branch_att, p_main, layer)
        xf, h = _output(m, w_out, xf, norm_g3, layer, last=layer == depth - 1)
    return xf.reshape(B, S, D)
```

```python
import functools
import math

import jax
import jax.numpy as jnp
from jax import lax
from jax.experimental import pallas as pl
from jax.experimental.pallas import tpu as pltpu

F32 = jnp.float32
BF16 = jnp.bfloat16

CHUNK = 64
EPS = 1e-6
RET_HEADS = 8
RET_DK = 256
RET_DV = 256
ROT_BASE = 10000.0
ATT_HEADS = 16
ATT_DH = 128
IDX_HEADS = 16
IDX_DH = 64
INDEX_TOPK = 256
Q_BLOCK = 128
LANES = 128
KEY_ROWS = 256

INT_MIN = -(2 ** 31)
KEY_NEG_INF = -2139095041
KEY_POS_INF = 2139095040
MASKED_LOGIT = -1e30
LOG2_E = 1.4426950408889634

VMEM_LIMIT_BYTES = 56 * 1024 * 1024


def _params(n_axes):
    return pltpu.CompilerParams(dimension_semantics=("arbitrary",) * n_axes,
                                vmem_limit_bytes=VMEM_LIMIT_BYTES)


def _tile(n, pref):
    t = min(n, pref)
    assert n % t == 0, (n, t)
    return t


def _sigmoid(g):
    return 0.5 * jnp.tanh(0.5 * g) + 0.5


def _silu(g):
    half = 0.5 * g
    return half * jnp.tanh(half) + half


def _fold_rows(x, op):
    SUBLANES = 8
    while x.shape[0] > SUBLANES and x.shape[0] % (2 * SUBLANES) == 0:
        half = x.shape[0] // 2
        x = op(x[:half], x[half:])
    return x


def _reduce_rows(x, op, finish):
    return finish(_fold_rows(x, op), axis=0, keepdims=True)


def _rmsnorm_kernel(x_ref, g_ref, h_ref):
    x = x_ref[...]
    r = lax.rsqrt(jnp.mean(x * x, axis=-1, keepdims=True) + EPS)
    h_ref[...] = (x * r * g_ref[...]).astype(h_ref.dtype)


def _rmsnorm(x, norm_g, layer):
    T, D = x.shape
    tm = _tile(T, 512)
    return pl.pallas_call(
        _rmsnorm_kernel,
        grid=(T // tm,),
        in_specs=[pl.BlockSpec((tm, D), lambda i: (i, 0)),
                  pl.BlockSpec((None, 1, D), lambda i: (layer, 0, 0))],
        out_specs=pl.BlockSpec((tm, D), lambda i: (i, 0)),
        out_shape=jax.ShapeDtypeStruct((T, D), BF16),
        compiler_params=_params(1),
        name="rmsnorm",
    )(x, norm_g)


OFF_AQ = 4 * RET_HEADS * RET_DK
OFF_AK = OFF_AQ + ATT_HEADS * ATT_DH
OFF_AV = OFF_AK + ATT_DH
OFF_AG = OFF_AV + ATT_DH
OFF_IQ = OFF_AG + ATT_HEADS * ATT_DH
OFF_IK = OFF_IQ + IDX_HEADS * IDX_DH
OFF_IW = OFF_IK + IDX_DH
OFF_GA = OFF_IW + IDX_HEADS
N_IN = OFF_GA + 2 * RET_HEADS * RET_DK


def _project_kernel(a_ref, *refs):
    *w_refs, o_ref, wb_ref = refs
    K = wb_ref.shape[0]
    STRIP = 512

    @pl.when(pl.program_id(1) == 0)
    def _():
        for k0 in range(0, K, STRIP):
            parts = [r[0, :, k0:k0 + STRIP] for r in w_refs]
            w = parts[0] if len(parts) == 1 else jnp.concatenate(parts, axis=0)
            wb_ref[k0:k0 + STRIP, :] = w.T.astype(BF16)

    o_ref[...] = jnp.dot(a_ref[...], wb_ref[...], preferred_element_type=F32).astype(o_ref.dtype)


def _project(h, w_t, windows, n_tiles, name):
    T, K = h.shape
    tm = _tile(T, 1024)
    tn = sum(rows for rows, _ in windows)
    specs = [pl.BlockSpec((pl.Element(1), pl.Element(rows), pl.Element(K)), index_map)
             for rows, index_map in windows]
    return pl.pallas_call(
        _project_kernel,
        grid=(n_tiles, T // tm),
        in_specs=[pl.BlockSpec((tm, K), lambda j, i: (i, 0))] + specs,
        out_specs=pl.BlockSpec((tm, tn), lambda j, i: (i, j)),
        out_shape=jax.ShapeDtypeStruct((T, n_tiles * tn), BF16),
        scratch_shapes=[pltpu.VMEM((K, tn), BF16)],
        compiler_params=_params(2),
        name=name,
    )(h, *([w_t] * len(specs)))


def _project_main(h, w_t, layer):
    tn = 1024
    SUB = 8
    runs = ((0, OFF_AK), (OFF_AG, OFF_IQ), (OFF_GA, N_IN))
    assert all(a % SUB == 0 and (b - a) % tn == 0 for a, b in runs)
    tiles_before = [sum((b - a) // tn for a, b in runs[:r]) for r in range(len(runs) + 1)]

    def first_row(j):
        row = j * (tn // SUB)
        for r in range(1, len(runs)):
            gap = runs[r][0] - runs[r - 1][1]
            row = row + jnp.where(j >= tiles_before[r], gap // SUB, 0)
        return row * SUB

    return _project(h, w_t, [(tn, lambda j, i: (layer, first_row(j), 0))], tiles_before[-1], "project_main")


def _project_small(h, w_t, layer):
    windows = [(IDX_HEADS * IDX_DH, lambda j, i: (layer, OFF_IQ, 0)),
               (2 * ATT_DH, lambda j, i: (layer, OFF_AK, 0)),
               (LANES, lambda j, i: (layer, OFF_IK, 0))]
    return _project(h, w_t, windows, 1, "project_small")


def _retention_tables(S, R):
    pos = jnp.arange(S, dtype=F32)
    inv = 1.0 / (ROT_BASE ** jnp.linspace(0.0, 1.0, RET_DK // 2, dtype=F32))
    ang = pos[:, None] * inv[None, :]
    log_g = jnp.log(1.0 - 2.0 ** (-5.0 - jnp.arange(RET_HEADS, dtype=F32)))[:, None, None]
    idx = jnp.arange(R, dtype=F32)
    chunk = jnp.arange(R) // CHUNK
    visible = chunk[None, :] <= chunk[:, None]
    k_scale = RET_DK ** -0.5
    assert k_scale == 2.0 ** round(math.log2(k_scale))
    dm = jnp.where(visible[None], jnp.exp(jnp.abs(idx[:, None] - idx[None, :])[None] * log_g), 0.0) * k_scale
    ones = jnp.ones((1, 1, RET_DV), F32)
    qd = jnp.exp((idx[None, :, None] + 1.0) * log_g) * ones
    kd = jnp.exp((R - 1.0 - idx)[None, :, None] * log_g) * ones * k_scale
    cd = jnp.exp(float(R) * log_g) * ones
    return jnp.cos(ang), jnp.sin(ang), dm, qd, kd, cd


def _retention_kernel(q_ref, k_ref, v_ref, g_ref, cos_ref, sin_ref, dm_ref, qd_ref, kd_ref, cd_ref,
                      og_ref, o_ref, state_ref):
    @pl.when(pl.program_id(1) == 0)
    def _():
        state_ref[...] = jnp.zeros_like(state_ref)

    cos = cos_ref[...]
    sin = sin_ref[...]
    half = RET_DK // 2

    def rotated(ref, h):
        x1 = ref[:, h * RET_DK:h * RET_DK + half].astype(F32)
        x2 = ref[:, h * RET_DK + half:(h + 1) * RET_DK].astype(F32)
        return jnp.concatenate([x1 * cos - x2 * sin, x2 * cos + x1 * sin], axis=1)

    for h in range(RET_HEADS):
        cols = slice(h * RET_DV, (h + 1) * RET_DV)
        qr = rotated(q_ref, h)
        kr = rotated(k_ref, h)
        qb = qr.astype(BF16)
        kb = kr.astype(BF16)
        v = v_ref[:, cols]
        s = lax.dot_general(qb, kb, (((1,), (1,)), ((), ())), preferred_element_type=F32) * dm_ref[h]
        o = jnp.dot(s.astype(BF16), v, preferred_element_type=F32)
        state = state_ref[h]
        o = o + jnp.dot(qb, state.astype(BF16), preferred_element_type=F32) * qd_ref[h]
        kdec = (kr * kd_ref[h]).astype(BF16)
        state_ref[h] = state * cd_ref[h] + lax.dot_general(
            kdec, v, (((0,), (0,)), ((), ())), preferred_element_type=F32)
        r = lax.rsqrt(jnp.mean(o * o, axis=-1, keepdims=True) + EPS)
        g = g_ref[:, cols].astype(F32)
        o_ref[:, cols] = (o * r * og_ref[h] * _silu(g)).astype(o_ref.dtype)


def _retention(p_main, tables, ret_out_g, layer, B, S):
    T = p_main.shape[0]
    W = RET_HEADS * RET_DV
    R = tables[2].shape[1]
    nr = S // R
    cos, sin, dm, qd, kd, cd = tables
    row = lambda b, i: b * nr + i
    full3 = lambda a: pl.BlockSpec(a.shape, lambda b, i: (0, 0, 0))
    return pl.pallas_call(
        _retention_kernel,
        grid=(B, nr),
        in_specs=[pl.BlockSpec((R, W), lambda b, i: (row(b, i), 0)),
                  pl.BlockSpec((R, W), lambda b, i: (row(b, i), 1)),
                  pl.BlockSpec((R, W), lambda b, i: (row(b, i), 2)),
                  pl.BlockSpec((R, W), lambda b, i: (row(b, i), 3)),
                  pl.BlockSpec((R, RET_DK // 2), lambda b, i: (i, 0)),
                  pl.BlockSpec((R, RET_DK // 2), lambda b, i: (i, 0)),
                  full3(dm), full3(qd), full3(kd), full3(cd),
                  pl.BlockSpec((None, RET_HEADS, 1, RET_DV), lambda b, i: (layer, 0, 0, 0))],
        out_specs=pl.BlockSpec((R, W), lambda b, i: (row(b, i), 0)),
        out_shape=jax.ShapeDtypeStruct((T, W), BF16),
        scratch_shapes=[pltpu.VMEM((RET_HEADS, RET_DK, RET_DV), F32)],
        compiler_params=_params(2),
        name="retention",
    )(p_main, p_main, p_main, p_main, cos, sin, dm, qd, kd, cd, ret_out_g)


def _prepare_keys_kernel(ak_ref, av_ref, ik_ref, gk_ref, gik_ref, kn_ref, vt_ref, ikn_ref):
    ak = ak_ref[...].astype(F32)
    r = lax.rsqrt(jnp.mean(ak * ak, axis=-1, keepdims=True) + EPS)
    kn_ref[...] = (ak * r * gk_ref[...]).astype(BF16)
    vt_ref[...] = av_ref[...].astype(F32).T.astype(BF16)
    ik = ik_ref[...].astype(F32)
    ik = jnp.where(lax.broadcasted_iota(jnp.int32, ik.shape, 1) < IDX_DH, ik, 0.0)
    r = lax.rsqrt(jnp.sum(ik * ik, axis=-1, keepdims=True) * (1.0 / IDX_DH) + EPS)
    ikn_ref[...] = (ik * r * gik_ref[...]).astype(BF16)


def _prepare_keys(p_small, gk, gik, layer, B, S):
    seg = IDX_HEADS * IDX_DH // LANES
    return pl.pallas_call(
        _prepare_keys_kernel,
        grid=(B,),
        in_specs=[pl.BlockSpec((S, LANES), lambda b: (b, seg)),
                  pl.BlockSpec((S, LANES), lambda b: (b, seg + 1)),
                  pl.BlockSpec((S, LANES), lambda b: (b, seg + 2)),
                  pl.BlockSpec((None, 1, ATT_DH), lambda b: (layer, 0, 0)),
                  pl.BlockSpec((None, 1, LANES), lambda b: (layer, 0, 0))],
        out_specs=[pl.BlockSpec((None, S, ATT_DH), lambda b: (b, 0, 0)),
                   pl.BlockSpec((None, ATT_DH, S), lambda b: (b, 0, 0)),
                   pl.BlockSpec((None, S, LANES), lambda b: (b, 0, 0))],
        out_shape=[jax.ShapeDtypeStruct((B, S, ATT_DH), BF16),
                   jax.ShapeDtypeStruct((B, ATT_DH, S), BF16),
                   jax.ShapeDtypeStruct((B, S, LANES), BF16)],
        compiler_params=_params(1),
        name="prepare_keys",
    )(p_small, p_small, p_small, gk, gik)


def _sparse_attention_kernel(iq_ref, iw_ref, aq_ref, ag_ref, kn_ref, vt_ref, ikn_ref, gq_ref, *rest,
                             topk, first_block):
    o_ref, key_ref, bias_ref, qt_ref, s_ref, p_ref, ot_ref = rest[-7:]
    S = kn_ref.shape[0]
    TQ = Q_BLOCK
    n_sub = iq_ref.shape[0] // TQ
    n_bits = 31
    blocks = [slice(c * KEY_ROWS, (c + 1) * KEY_ROWS) for c in range(S // KEY_ROWS)]

    def scores(sub):
        q_rows = slice(sub * TQ, (sub + 1) * TQ)
        iqT = iq_ref[q_rows, :].astype(F32).T.astype(BF16)
        wT = iw_ref[q_rows, :].astype(F32).T[IDX_DH:IDX_DH + IDX_HEADS, :] * (IDX_HEADS ** -0.5 * IDX_DH ** -0.5)
        no_rows = jnp.zeros((LANES - IDX_DH, 2 * TQ), BF16)
        score = jnp.zeros((S, TQ), F32)
        for hp in range(IDX_HEADS // 2):
            h0, h1 = 2 * hp, 2 * hp + 1
            q_pair = jnp.concatenate([iqT[h0 * IDX_DH:(h0 + 1) * IDX_DH, :],
                                      iqT[h1 * IDX_DH:(h1 + 1) * IDX_DH, :]], axis=1)
            s = jnp.dot(ikn_ref[...], jnp.concatenate([q_pair, no_rows], axis=0), preferred_element_type=F32)
            score = score + jnp.maximum(s[:, :TQ], 0.0) * wT[h0:h0 + 1, :]
            score = score + jnp.maximum(s[:, TQ:], 0.0) * wT[h1:h1 + 1, :]
        bits = lax.bitcast_convert_type(score, jnp.int32)
        key = bits ^ ((bits >> 31) & 0x7FFFFFFF)
        first_query = (first_block + pl.program_id(1) * n_sub + sub) * TQ
        kpos = lax.broadcasted_iota(jnp.int32, (S, TQ), 0)
        qpos = lax.broadcasted_iota(jnp.int32, (S, TQ), 1) + first_query
        key_ref[sub] = jnp.where(kpos // CHUNK <= qpos // CHUNK, key, INT_MIN)

    def count(sub, pred):
        acc = jnp.zeros((KEY_ROWS, TQ), F32)
        for c, rows in enumerate(blocks):
            kpos = lax.broadcasted_iota(jnp.int32, (KEY_ROWS, TQ), 0) + c * KEY_ROWS
            acc = acc + jnp.where(pred(key_ref[sub, rows, :], kpos), 1.0, 0.0)
        return _reduce_rows(acc, jnp.add, jnp.sum)

    def try_threshold(sub, cand, thr, cnt):
        c = count(sub, lambda key, kpos: key >= cand)
        ok = c >= topk
        return jnp.where(ok, cand, thr), jnp.where(ok, c, cnt)

    def sign_step(sub):
        thr = jnp.full((1, TQ), INT_MIN, jnp.int32)
        cnt = jnp.full((1, TQ), float(S), F32)
        return try_threshold(sub, jnp.zeros((1, TQ), jnp.int32), thr, cnt)

    def bit_step(sub, it, thr, cnt):
        return try_threshold(sub, thr | (1 << (n_bits - 1 - it)), thr, cnt)

    def selected_bias(sub, selected):
        for c, rows in enumerate(blocks):
            key = key_ref[sub, rows, :]
            kpos = lax.broadcasted_iota(jnp.int32, (KEY_ROWS, TQ), 0) + c * KEY_ROWS
            finite = (key > KEY_NEG_INF) & (key < KEY_POS_INF)
            b = jnp.where(selected(key, kpos) & finite, 0.0, MASKED_LOGIT)
            bias_ref[sub, rows, :] = b.astype(BF16)

    def select(sub, thr, cnt):
        selected_bias(sub, lambda key, kpos: key >= thr)
        tied = jnp.where((cnt > topk) & (thr > KEY_NEG_INF), 1.0, 0.0)

        @pl.when(jnp.max(tied) > 0.0)
        def _break_ties():
            need = topk - count(sub, lambda key, kpos: key > thr)

            def pos_step(it, bound):
                cand = bound | (1 << (S.bit_length() - 1 - it))
                ok = count(sub, lambda key, kpos: (key == thr) & (kpos < cand)) <= need
                return jnp.where(ok, cand, bound)

            bound = lax.fori_loop(0, S.bit_length(), pos_step, jnp.zeros((1, TQ), jnp.int32))
            selected_bias(sub, lambda key, kpos: (key > thr) | ((key == thr) & (kpos < bound)))

    def normalise_queries(sub):
        aqT = aq_ref[sub * TQ:(sub + 1) * TQ, :].astype(F32).T
        gq = gq_ref[...] * (ATT_DH ** -0.5 * LOG2_E)
        for h in range(ATT_HEADS):
            x = aqT[h * ATT_DH:(h + 1) * ATT_DH, :]
            r = lax.rsqrt(_reduce_rows(x * x, jnp.add, jnp.sum) * (1.0 / ATT_DH) + EPS)
            qt_ref[sub, h * ATT_DH:(h + 1) * ATT_DH, :] = (x * r * gq).astype(BF16)

    n_pairs = ATT_HEADS // 2
    eye = (lax.broadcasted_iota(jnp.int32, (TQ, TQ), 0) == lax.broadcasted_iota(jnp.int32, (TQ, TQ), 1))
    eye = jnp.where(eye, 1.0, 0.0).astype(BF16)
    eye_pair = jnp.concatenate([eye, eye], axis=1)

    def attention(sub):
        def logits_stage(hp, rows, q_pair, top):
            s = jnp.dot(jnp.concatenate([kn_ref[rows, :], bias_ref[sub, rows, :]], axis=1),
                        jnp.concatenate([q_pair, eye_pair], axis=0), preferred_element_type=F32)
            s_ref[hp % 2, rows, :] = s
            return jnp.maximum(top, _fold_rows(s, jnp.maximum))

        def probs_stage(hp, rows, top, den):
            p = jnp.exp2(s_ref[hp % 2, rows, :] - top)
            p_ref[hp % 2, rows, :] = p.astype(BF16)
            return den + _fold_rows(p, jnp.add)

        def values_stage(hp, rows, acc):
            return acc + jnp.dot(vt_ref[:, rows], p_ref[hp % 2, rows, :], preferred_element_type=F32)

        def query_pair(hp):
            row0 = hp * 2 * ATT_DH
            return jnp.concatenate([qt_ref[sub, row0:row0 + ATT_DH, :],
                                    qt_ref[sub, row0 + ATT_DH:row0 + 2 * ATT_DH, :]], axis=1)

        def finish(hp, acc, den):
            oT = acc / den
            row0 = hp * 2 * ATT_DH
            ot_ref[sub, row0:row0 + ATT_DH, :] = oT[:, :TQ]
            ot_ref[sub, row0 + ATT_DH:row0 + 2 * ATT_DH, :] = oT[:, TQ:]

        neg = jnp.full((8, 2 * TQ), -jnp.inf, F32)
        q_next = query_pair(0)
        top_next = neg
        for rows in blocks:
            top_next = logits_stage(0, rows, q_next, top_next)
        den_prev = None
        for hp in range(n_pairs):
            top = jnp.max(top_next, axis=0, keepdims=True)
            if hp + 1 < n_pairs:
                q_next = query_pair(hp + 1)
                top_next = neg
            den = jnp.zeros((8, 2 * TQ), F32)
            acc = jnp.zeros((ATT_DH, 2 * TQ), F32)
            for rows in blocks:
                if hp + 1 < n_pairs:
                    top_next = logits_stage(hp + 1, rows, q_next, top_next)
                den = probs_stage(hp, rows, top, den)
                if hp > 0:
                    acc = values_stage(hp - 1, rows, acc)
            if hp > 0:
                finish(hp - 1, acc, den_prev)
            den_prev = jnp.sum(den, axis=0, keepdims=True)
        acc = jnp.zeros((ATT_DH, 2 * TQ), F32)
        for rows in blocks:
            acc = values_stage(n_pairs - 1, rows, acc)
        finish(n_pairs - 1, acc, den_prev)

    for sub in range(n_sub):
        scores(sub)
    if S <= topk:
        found = ((jnp.full((1, TQ), INT_MIN, jnp.int32), jnp.full((1, TQ), float(S), F32)),) * n_sub
    else:
        found = tuple(sign_step(sub) for sub in range(n_sub))
        found = lax.fori_loop(
            0, n_bits, lambda it, found: tuple(bit_step(sub, it, *found[sub]) for sub in range(n_sub)), found)
    for sub in range(n_sub):
        select(sub, *found[sub])
        normalise_queries(sub)
    for sub in range(n_sub):
        attention(sub)

    for sub in range(n_sub):
        q_rows = slice(sub * TQ, (sub + 1) * TQ)
        o_ref[q_rows, :] = (ot_ref[sub].T * _silu(ag_ref[q_rows, :].astype(F32))).astype(o_ref.dtype)


def _sparse_attention(p_main, p_small, keys, gq, out, layer, B, S):
    T = p_main.shape[0]
    W = ATT_HEADS * ATT_DH
    TQ = Q_BLOCK
    nb = S // TQ
    topk = min(INDEX_TOPK, S // 4)
    n_iq = IDX_HEADS * IDX_DH
    seg = n_iq // LANES
    kn, vt, ikn = keys
    group_keys = max(KEY_ROWS, S // 8)
    group_blocks = group_keys // TQ
    n_sub = 2
    assert group_blocks % n_sub == 0
    rows = n_sub * TQ
    assert out.shape == (T, W) and out.dtype == BF16
    for first_block in range(0, nb, group_blocks):
        n_keys = (first_block + group_blocks) * TQ
        qrow = lambda b, i, first_block=first_block: (b * nb + first_block) // n_sub + i
        in_specs = [pl.BlockSpec((rows, n_iq), lambda b, i: (qrow(b, i), 0)),
                    pl.BlockSpec((rows, LANES), lambda b, i: (qrow(b, i), seg + 2)),
                    pl.BlockSpec((rows, W), lambda b, i: (qrow(b, i), 4)),
                    pl.BlockSpec((rows, W), lambda b, i: (qrow(b, i), 5)),
                    pl.BlockSpec((None, n_keys, ATT_DH), lambda b, i: (b, 0, 0)),
                    pl.BlockSpec((None, ATT_DH, n_keys), lambda b, i: (b, 0, 0)),
                    pl.BlockSpec((None, n_keys, LANES), lambda b, i: (b, 0, 0)),
                    pl.BlockSpec((None, ATT_DH, 1), lambda b, i: (layer, 0, 0)),
                    pl.BlockSpec(memory_space=pl.ANY)]
        args = [p_small, p_small, p_main, p_main, kn, vt, ikn, gq, out]
        out = pl.pallas_call(
            functools.partial(_sparse_attention_kernel, topk=topk, first_block=first_block),
            grid=(B, group_blocks // n_sub),
            in_specs=in_specs,
            out_specs=pl.BlockSpec((rows, W), lambda b, i: (qrow(b, i), 0)),
            out_shape=jax.ShapeDtypeStruct((T, W), BF16),
            scratch_shapes=[pltpu.VMEM((n_sub, n_keys, TQ), jnp.int32),
                            pltpu.VMEM((n_sub, n_keys, TQ), BF16),
                            pltpu.VMEM((n_sub, W, TQ), BF16),
                            pltpu.VMEM((2, n_keys, 2 * TQ), F32),
                            pltpu.VMEM((2, n_keys, 2 * TQ), BF16),
                            pltpu.VMEM((n_sub, W, TQ), F32)],
            input_output_aliases={len(args) - 1: 0},
            compiler_params=_params(2),
            name="sparse_attention",
        )(*args)
    return out


def _merge_kernel(ar_ref, aa_ref, wr_ref, wa_ref, ga_ref, gb_ref, m_ref, wrb_ref, wab_ref):
    @pl.when(pl.program_id(1) == 0)
    def _():
        wrb_ref[...] = wr_ref[...].astype(BF16)
        wab_ref[...] = wa_ref[...].astype(BF16)

    u_ret = jnp.dot(ar_ref[...], wrb_ref[...], preferred_element_type=F32)
    u_att = jnp.dot(aa_ref[...], wab_ref[...], preferred_element_type=F32)
    m = _sigmoid(ga_ref[...].astype(F32)) * u_ret + _sigmoid(gb_ref[...].astype(F32)) * u_att
    m_ref[...] = m.astype(m_ref.dtype)


def _merge(a_ret, a_att, w_ret, w_att, p_main, layer):
    T, K = a_ret.shape
    D = w_ret.shape[-1]
    tm = _tile(T, 512)
    tn = _tile(D, 1024)
    gate_a = 6 * (D // tn)
    gate_b = 7 * (D // tn)
    w_spec = pl.BlockSpec((None, K, tn), lambda j, i: (layer, 0, j), pipeline_mode=pl.Buffered(1))
    return pl.pallas_call(
        _merge_kernel,
        grid=(D // tn, T // tm),
        in_specs=[pl.BlockSpec((tm, K), lambda j, i: (i, 0)),
                  pl.BlockSpec((tm, K), lambda j, i: (i, 0)),
                  w_spec, w_spec,
                  pl.BlockSpec((tm, tn), lambda j, i: (i, gate_a + j)),
                  pl.BlockSpec((tm, tn), lambda j, i: (i, gate_b + j))],
        out_specs=pl.BlockSpec((tm, tn), lambda j, i: (i, j)),
        out_shape=jax.ShapeDtypeStruct((T, D), BF16),
        scratch_shapes=[pltpu.VMEM((K, tn), BF16), pltpu.VMEM((K, tn), BF16)],
        compiler_params=_params(2),
        name="merge",
    )(a_ret, a_att, w_ret, w_att, p_main, p_main)


def _output_kernel(m_ref, w_ref, x_ref, *refs):
    wb_ref = refs[-1]

    @pl.when(pl.program_id(0) == 0)
    def _():
        wb_ref[...] = w_ref[...].astype(BF16)

    x = x_ref[...] + jnp.dot(m_ref[...], wb_ref[...], preferred_element_type=F32)
    if len(refs) == 2:
        refs[0][...] = x
    else:
        g_ref, xo_ref, h_ref = refs[:3]
        xo_ref[...] = x
        r = lax.rsqrt(jnp.mean(x * x, axis=-1, keepdims=True) + EPS)
        h_ref[...] = (x * r * g_ref[...]).astype(h_ref.dtype)


def _output(m, w_out, x, norm_g, layer, last):
    T, D = x.shape
    tm = _tile(T, 512)
    row_spec = pl.BlockSpec((tm, D), lambda i: (i, 0))
    w_spec = pl.BlockSpec((None, D, D), lambda i: (layer, 0, 0), pipeline_mode=pl.Buffered(1))
    in_specs = [row_spec, w_spec, row_spec]
    scratch = [pltpu.VMEM((D, D), BF16)]
    if last:
        return pl.pallas_call(
            _output_kernel, grid=(T // tm,), in_specs=in_specs, out_specs=row_spec,
            out_shape=jax.ShapeDtypeStruct((T, D), F32), scratch_shapes=scratch,
            compiler_params=_params(1), name="output_last",
        )(m, w_out, x), None
    return pl.pallas_call(
        _output_kernel, grid=(T // tm,),
        in_specs=in_specs + [pl.BlockSpec((None, 1, D), lambda i: (layer + 1, 0, 0))],
        out_specs=[row_spec, row_spec],
        out_shape=[jax.ShapeDtypeStruct((T, D), F32), jax.ShapeDtypeStruct((T, D), BF16)],
        scratch_shapes=scratch, compiler_params=_params(1), name="output",
    )(m, w_out, x, norm_g)


def kernel(x, norm_g, w_in, ret_out_g, att_q_g, att_k_g, idx_k_g, w_branch_ret, w_branch_att, w_out):
    B, S, D = x.shape
    depth = w_in.shape[0]
    T = B * S
    assert w_in.shape[-1] == N_IN and S % KEY_ROWS == 0
    norm_g3 = norm_g.reshape(depth, 1, D)
    ret_g4 = ret_out_g.reshape(depth, RET_HEADS, 1, RET_DV)
    gq = att_q_g.reshape(depth, ATT_DH, 1)
    gk = att_k_g.reshape(depth, 1, ATT_DH)
    gik = jnp.pad(idx_k_g, ((0, 0), (0, LANES - IDX_DH))).reshape(depth, 1, LANES)
    tables = _retention_tables(S, _tile(S, 256))
    w_t = jnp.swapaxes(w_in, 1, 2)

    xf = x.reshape(T, D)
    h = _rmsnorm(xf, norm_g3, 0)
    for layer in range(depth):
        p_main = _project_main(h, w_t, layer)
        p_small = _project_small(h, w_t, layer)
        a_ret = _retention(p_main, tables, ret_g4, layer, B, S)
        keys = _prepare_keys(p_small, gk, gik, layer, B, S)
        a_att = _sparse_attention(p_main, p_small, keys, gq, h, layer, B, S)
        m = _merge(a_ret, a_att, w_branch_ret, w_branch_att, p_main, layer)
        xf, h = _output(m, w_out, xf, norm_g3, layer, last=layer == depth - 1)
    return xf.reshape(B, S, D)
```
